```python
import math
import jax, jax.numpy as jnp
from jax import lax
import numpy as np

D_MODEL = 1024
BATCH = 8
SEQ = 2048
DEPTH = 2
DEC_BATCH = 4
DEC_SEQ = 8192
PAST_LEN = 128

N_DIR = 2
EPS = 1e-6
S5_WIDTH = 3 * D_MODEL // 4
S5_GROUP = 16
S5_GROUPS = S5_WIDTH // S5_GROUP
S5_STATE = 64
M2_INNER = 3 * D_MODEL // 2
M2_HEADDIM = 64
M2_HEADS = M2_INNER // M2_HEADDIM
M2_GROUPS = 4
M2_STATE = 128
M2_CONV = 4
M2_PAD_L = (M2_CONV - 1) // 2
M2_PAD_R = M2_CONV - 1 - M2_PAD_L
M2_CHUNK = 128
M2_GN = M2_GROUPS * M2_STATE
M2_CONV_DIM = M2_INNER + 2 * M2_GN
D_FF = 4 * D_MODEL
OFF_U = 0
OFF_Z = OFF_U + S5_WIDTH
OFF_XBC = OFF_Z + M2_INNER
OFF_DT = OFF_XBC + M2_CONV_DIM
OFF_GATE = OFF_DT + N_DIR * M2_HEADS
D_IN_PROJ = OFF_GATE + 2 * D_MODEL

kernel_name = 'hybrid_s5_ssd_gated_encoder'


def rmsnorm(x, w):
    xf = x.astype(jnp.float32)
    y = xf * lax.rsqrt(jnp.mean(jnp.square(xf), axis=-1, keepdims=True) + EPS)
    return (y * w.astype(jnp.float32)).astype(x.dtype)


def gated_rmsnorm(y, z, w):
    g = y * jax.nn.silu(z.astype(jnp.float32))
    shp = g.shape
    g = g.reshape(shp[:-1] + (M2_GROUPS, M2_INNER // M2_GROUPS))
    g = g * lax.rsqrt(jnp.mean(jnp.square(g), axis=-1, keepdims=True) + EPS)
    return g.reshape(shp) * w.astype(jnp.float32)


def complex_linear_scan(a_re, a_im, b_re, b_im, reverse):
    a_re = jnp.broadcast_to(a_re, b_re.shape)
    a_im = jnp.broadcast_to(a_im, b_re.shape)

    def combine(e1, e2):
        a1r, a1i, b1r, b1i = e1
        a2r, a2i, b2r, b2i = e2
        return (a2r * a1r - a2i * a1i,
                a2r * a1i + a2i * a1r,
                a2r * b1r - a2i * b1i + b2r,
                a2r * b1i + a2i * b1r + b2i)

    _, _, s_re, s_im = lax.associative_scan(combine, (a_re, a_im, b_re, b_im),
                                            reverse=reverse, axis=1)
    return s_re, s_im


def s5_mixer(u, lam_re, lam_im, log_dt, b_re, b_im, c_re, c_im, d_skip, w_glu, b_glu):
    f32 = jnp.float32
    bsz, L, _ = u.shape
    uf = u.astype(f32)
    ug = uf.reshape(bsz, L, S5_GROUPS, S5_GROUP)
    y = d_skip.astype(f32) * uf
    for d in range(N_DIR):
        lr = lam_re[d].astype(f32)
        li = lam_im[d].astype(f32)
        delta = jnp.exp(log_dt[d].astype(f32))[:, None]
        mag = jnp.exp(lr * delta)
        a_re = mag * jnp.cos(li * delta)
        a_im = mag * jnp.sin(li * delta)
        inv = 1.0 / (lr * lr + li * li)
        q_re = ((a_re - 1.0) * lr + a_im * li) * inv
        q_im = (a_im * lr - (a_re - 1.0) * li) * inv
        br = b_re[d].astype(f32)
        bi = b_im[d].astype(f32)
        bbar_re = q_re[..., None] * br - q_im[..., None] * bi
        bbar_im = q_re[..., None] * bi + q_im[..., None] * br
        bu_re = jnp.einsum('blgc,gpc->blgp', ug, bbar_re)
        bu_im = jnp.einsum('blgc,gpc->blgp', ug, bbar_im)
        s_re, s_im = complex_linear_scan(a_re, a_im, bu_re, bu_im, reverse=(d == 1))
        y_dir = (jnp.einsum('gcp,blgp->blgc', c_re[d].astype(f32), s_re)
                 - jnp.einsum('gcp,blgp->blgc', c_im[d].astype(f32), s_im))
        y = y + y_dir.reshape(bsz, L, S5_WIDTH)
    h = jax.nn.gelu(y)
    return h * jax.nn.sigmoid(h @ w_glu.astype(f32) + b_glu.astype(f32))


def ssd_scan(x, dt, a, b_in, c_in):
    bsz, L = x.shape[:2]
    nc = L // M2_CHUNK
    k = M2_HEADS // M2_GROUPS
    x = x.reshape(bsz, nc, M2_CHUNK, M2_GROUPS, k, M2_HEADDIM)
    dt = dt.reshape(bsz, nc, M2_CHUNK, M2_GROUPS, k)
    bm = b_in.reshape(bsz, nc, M2_CHUNK, M2_GROUPS, M2_STATE)
    cm = c_in.reshape(bsz, nc, M2_CHUNK, M2_GROUPS, M2_STATE)
    da_cs = jnp.cumsum(dt * a.reshape(M2_GROUPS, k), axis=2)
    xdt = x * dt[..., None]
    lower = jnp.tril(jnp.ones((M2_CHUNK, M2_CHUNK), dtype=bool))[:, :, None, None]
    seg = da_cs[:, :, :, None] - da_cs[:, :, None, :]
    decay = jnp.exp(jnp.where(lower, seg, -jnp.inf))
    cb = jnp.einsum('bclgn,bcsgn->bclsg', cm, bm)
    y_diag = jnp.einsum('bclsg,bclsgk,bcsgkp->bclgkp', cb, decay, xdt)
    decay_in = jnp.exp(da_cs[:, :, -1:] - da_cs)
    chunk_states = jnp.einsum('bcsgn,bcsgk,bcsgkp->bcgkpn', bm, decay_in, xdt)
    chunk_decay = jnp.exp(da_cs[:, :, -1])

    def step(carry, inp):
        st, dec = inp
        return carry * dec[..., None, None] + st, carry

    init = jnp.zeros((bsz, M2_GROUPS, k, M2_HEADDIM, M2_STATE), jnp.float32)
    _, prev = lax.scan(step, init, (jnp.moveaxis(chunk_states, 1, 0),
                                    jnp.moveaxis(chunk_decay, 1, 0)))
    prev = jnp.moveaxis(prev, 0, 1)
    y_off = jnp.einsum('bclgn,bcgkpn,bclgk->bclgkp', cm, prev, jnp.exp(da_cs))
    return (y_diag + y_off).reshape(bsz, L, M2_HEADS, M2_HEADDIM)


def mamba2_mixer(z, xbc, dt_raw, conv_w, conv_b, dt_bias, a_log, d_skip, norm_w):
    f32 = jnp.float32
    bsz, L, _ = z.shape
    xbc = lax.conv_general_dilated(xbc.astype(f32), conv_w.astype(f32)[:, None, :], (1,),
                                   [(M2_PAD_L, M2_PAD_R)],
                                   dimension_numbers=('NWC', 'WIO', 'NWC'),
                                   feature_group_count=M2_CONV_DIM)
    xbc = jax.nn.silu(xbc + conv_b.astype(f32))
    xs = xbc[..., :M2_INNER].reshape(bsz, L, M2_HEADS, M2_HEADDIM)
    bm = xbc[..., M2_INNER:M2_INNER + M2_GN].reshape(bsz, L, M2_GROUPS, M2_STATE)
    cm = xbc[..., M2_INNER + M2_GN:].reshape(bsz, L, M2_GROUPS, M2_STATE)
    dt_raw = dt_raw.astype(f32).reshape(bsz, L, N_DIR, M2_HEADS)
    y = d_skip.astype(f32)[:, None] * xs
    for d in range(N_DIR):
        dt = jax.nn.softplus(dt_raw[:, :, d] + dt_bias[d].astype(f32))
        a = -jnp.exp(a_log[d].astype(f32))
        if d == 0:
            y = y + ssd_scan(xs, dt, a, bm, cm)
        else:
            y = y + jnp.flip(ssd_scan(jnp.flip(xs, 1), jnp.flip(dt, 1), a,
                                      jnp.flip(bm, 1), jnp.flip(cm, 1)), 1)
    return gated_rmsnorm(y.reshape(bsz, L, M2_INNER), z, norm_w)


def encoder_layer(x, norm1_w, w_in, lam_re, lam_im, log_dt, b_re, b_im, c_re, c_im,
                  d_s5, w_glu, b_glu, w_s5_out, conv_w, conv_b, dt_bias, a_log, d_m2,
                  m2_norm_w, w_m2_out, w_o, norm2_w, w_up, w_down):
    f32 = jnp.float32
    h = rmsnorm(x, norm1_w)
    proj = h @ w_in
    u = proj[..., OFF_U:OFF_Z]
    z = proj[..., OFF_Z:OFF_XBC]
    xbc = proj[..., OFF_XBC:OFF_DT]
    dt_raw = proj[..., OFF_DT:OFF_GATE]
    gates = jax.nn.sigmoid(proj[..., OFF_GATE:].astype(f32))
    s5 = s5_mixer(u, lam_re, lam_im, log_dt, b_re, b_im, c_re, c_im, d_s5, w_glu, b_glu)
    s5 = s5 @ w_s5_out.astype(f32)
    m2 = mamba2_mixer(z, xbc, dt_raw, conv_w, conv_b, dt_bias, a_log, d_m2, m2_norm_w)
    m2 = m2 @ w_m2_out.astype(f32)
    merged = gates[..., :D_MODEL] * s5 + gates[..., D_MODEL:] * m2
    x = x + (merged @ w_o.astype(f32)).astype(x.dtype)
    h = rmsnorm(x, norm2_w)
    x = x + (jnp.square(jax.nn.relu(h @ w_up)) @ w_down).astype(x.dtype)
    return x


def trunk(x, norm1_w, w_in, lam_re, lam_im, log_dt, b_re, b_im, c_re, c_im, d_s5, w_glu,
          b_glu, w_s5_out, conv_w, conv_b, dt_bias, a_log, d_m2, m2_norm_w, w_m2_out, w_o,
          norm2_w, w_up, w_down, final_norm_w):
    for i in range(DEPTH):
        x = encoder_layer(x, norm1_w[i], w_in[i], lam_re[i], lam_im[i], log_dt[i], b_re[i],
                          b_im[i], c_re[i], c_im[i], d_s5[i], w_glu[i], b_glu[i],
                          w_s5_out[i], conv_w[i], conv_b[i], dt_bias[i], a_log[i], d_m2[i],
                          m2_norm_w[i], w_m2_out[i], w_o[i], norm2_w[i], w_up[i], w_down[i])
    return rmsnorm(x, final_norm_w)


def setup_inputs(seed: int = 0) -> dict:
    key = jax.random.key(seed)
    ks = jax.random.split(key, 32)
    f32 = jnp.float32
    nrm = lambda k, shp, s: jax.random.normal(k, shp, f32) * s
    G, P = S5_GROUPS, S5_STATE
    lam_im_base = math.pi * jnp.arange(P, dtype=f32)
    dt_m2 = jnp.exp(jax.random.uniform(ks[16], (DEPTH, N_DIR, M2_HEADS), f32,
                                       math.log(1e-3), math.log(1e-1)))
    return {
        'x_prompt': nrm(ks[0], (BATCH, SEQ, D_MODEL), 1.0),
        'x_sample': nrm(ks[1], (DEC_BATCH, DEC_SEQ, D_MODEL), 1.0),
        'norm1_w': 1.0 + nrm(ks[2], (DEPTH, D_MODEL), 0.01),
        'w_in': nrm(ks[3], (DEPTH, D_MODEL, D_IN_PROJ), D_MODEL ** -0.5),
        'lam_re': -0.5 + nrm(ks[4], (DEPTH, N_DIR, G, P), 0.01),
        'lam_im': lam_im_base + nrm(ks[5], (DEPTH, N_DIR, G, P), 0.01),
        'log_dt': jax.random.uniform(ks[6], (DEPTH, N_DIR, G), f32,
                                     math.log(1e-3), math.log(1e-1)),
        'b_re': nrm(ks[7], (DEPTH, N_DIR, G, P, S5_GROUP), (2 * S5_GROUP) ** -0.5),
        'b_im': nrm(ks[8], (DEPTH, N_DIR, G, P, S5_GROUP), (2 * S5_GROUP) ** -0.5),
        'c_re': nrm(ks[9], (DEPTH, N_DIR, G, S5_GROUP, P), P ** -0.5),
        'c_im': nrm(ks[10], (DEPTH, N_DIR, G, S5_GROUP, P), P ** -0.5),
        'd_s5': nrm(ks[11], (DEPTH, S5_WIDTH), 1.0),
        'w_glu': nrm(ks[12], (DEPTH, S5_WIDTH, S5_WIDTH), S5_WIDTH ** -0.5),
        'b_glu': nrm(ks[13], (DEPTH, S5_WIDTH), 0.01),
        'w_s5_out': nrm(ks[14], (DEPTH, S5_WIDTH, D_MODEL), S5_WIDTH ** -0.5),
        'conv_w': nrm(ks[15], (DEPTH, M2_CONV, M2_CONV_DIM), M2_CONV ** -0.5),
        'conv_b': nrm(ks[17], (DEPTH, M2_CONV_DIM), 0.01),
        'dt_bias': dt_m2 + jnp.log(-jnp.expm1(-dt_m2)),
        'a_log': jnp.log(jax.random.uniform(ks[18], (DEPTH, N_DIR, M2_HEADS), f32, 1.0, 16.0)),
        'd_m2': 1.0 + nrm(ks[19], (DEPTH, M2_HEADS), 0.01),
        'm2_norm_w': 1.0 + nrm(ks[20], (DEPTH, M2_INNER), 0.01),
        'w_m2_out': nrm(ks[21], (DEPTH, M2_INNER, D_MODEL), M2_INNER ** -0.5),
        'w_o': nrm(ks[22], (DEPTH, D_MODEL, D_MODEL), D_MODEL ** -0.5),
        'norm2_w': 1.0 + nrm(ks[23], (DEPTH, D_MODEL), 0.01),
        'w_up': nrm(ks[24], (DEPTH, D_MODEL, D_FF), D_MODEL ** -0.5),
        'w_down': nrm(ks[25], (DEPTH, D_FF, D_MODEL), D_FF ** -0.5),
        'final_norm_w': 1.0 + nrm(ks[26], (D_MODEL,), 0.01),
    }


def reference(x_prompt, x_sample, norm1_w, w_in, lam_re, lam_im, log_dt, b_re, b_im, c_re,
              c_im, d_s5, w_glu, b_glu, w_s5_out, conv_w, conv_b, dt_bias, a_log, d_m2,
              m2_norm_w, w_m2_out, w_o, norm2_w, w_up, w_down, final_norm_w):
    y_prompt = trunk(x_prompt, norm1_w, w_in, lam_re, lam_im, log_dt, b_re, b_im, c_re, c_im,
                     d_s5, w_glu, b_glu, w_s5_out, conv_w, conv_b, dt_bias, a_log, d_m2,
                     m2_norm_w, w_m2_out, w_o, norm2_w, w_up, w_down, final_norm_w)
    y_sample = trunk(x_sample, norm1_w, w_in, lam_re, lam_im, log_dt, b_re, b_im, c_re, c_im,
                     d_s5, w_glu, b_glu, w_s5_out, conv_w, conv_b, dt_bias, a_log, d_m2,
                     m2_norm_w, w_m2_out, w_o, norm2_w, w_up, w_down, final_norm_w)
    return (y_prompt, y_sample)
```

```python
import functools

import jax
import jax.numpy as jnp
from jax import lax
from jax.experimental import pallas as pl
from jax.experimental.pallas import tpu as pltpu

F32 = jnp.float32
BF16 = jnp.bfloat16

D_MODEL = 1024
DEPTH = 2
N_DIR = 2
EPS = 1e-6
S5_WIDTH = 768
S5_GROUP = 16
S5_GROUPS = 48
S5_STATE = 64
S5_LANES = S5_GROUPS * S5_STATE
M2_INNER = 1536
M2_HEADDIM = 64
M2_HEADS = 24
M2_GROUPS = 4
M2_STATE = 128
M2_CONV = 4
M2_GN = M2_GROUPS * M2_STATE
M2_CONV_DIM = M2_INNER + 2 * M2_GN
HEADS_PER_GROUP = M2_HEADS // M2_GROUPS
D_FF = 4096

LANES = 128
SUBLANES = 8
VMEM_LIMIT_BYTES = 56 * 1024 * 1024

DT_PAD = LANES
S5_SLAB_GROUPS = LANES // S5_GROUP
S5_SLABS = S5_WIDTH // LANES
S5_SLAB_STATES = S5_SLAB_GROUPS * S5_STATE

TM_INPROJ = 256
TM_MIX = 256
TM_MLP = 512
S5_CHUNK = 256
SSD_CHUNK = 128
CONV_TILE = 512
N_CHUNK_COLS = 512


def _cparams():
    return pltpu.CompilerParams(dimension_semantics=None, vmem_limit_bytes=VMEM_LIMIT_BYTES)


def _resident(shape):
    nd = len(shape)
    return pl.BlockSpec(shape, lambda *_: (0,) * nd, pipeline_mode=pl.Buffered(1))


def _silu(x):
    return x * (1.0 / (1.0 + jnp.exp(-x)))


def _sigmoid(x):
    return 1.0 / (1.0 + jnp.exp(-x))


def _softplus(x):
    return jnp.maximum(x, 0.0) + jnp.log1p(jnp.exp(-jnp.abs(x)))


def _rms(x, w):
    var = jnp.mean(x * x, axis=-1, keepdims=True)
    return x * lax.rsqrt(var + EPS) * w


def _dot(a, b):
    return jnp.dot(a, b, preferred_element_type=F32)


def _split3(x):
    hi = x.astype(BF16)
    r1 = x - hi.astype(F32)
    mid = r1.astype(BF16)
    lo = (r1 - mid.astype(F32)).astype(BF16)
    return hi, mid, lo


def _inproj_kernel(x_ref, nw_ref, w_ref, u_ref, z_ref, xbc_ref, dt_ref, g_ref):
    h = _rms(x_ref[...], nw_ref[...]).astype(BF16)
    off = 0
    for ref in (u_ref, z_ref, xbc_ref, dt_ref, g_ref):
        n = ref.shape[-1]
        for c0 in range(0, n, N_CHUNK_COLS):
            c1 = min(c0 + N_CHUNK_COLS, n)
            ref[:, c0:c1] = _dot(h, w_ref[:, off + c0:off + c1])
        off += n


def _inproj(x2d, norm_w, w_in_p):
    t = x2d.shape[0]
    tm = TM_INPROJ
    widths = (S5_WIDTH, M2_INNER, M2_CONV_DIM, DT_PAD, 2 * D_MODEL)
    row = lambda n: pl.BlockSpec((tm, n), lambda i: (i, 0))
    return pl.pallas_call(
        _inproj_kernel,
        grid=(t // tm,),
        in_specs=[row(D_MODEL), _resident((1, D_MODEL)), _resident(w_in_p.shape)],
        out_specs=[row(n) for n in widths],
        out_shape=[jax.ShapeDtypeStruct((t, n), F32) for n in widths],
        compiler_params=_cparams(),
        name="inproj",
    )(x2d, norm_w, w_in_p)


def _s5_kernel(uf_ref, ub_ref, bre_ref, bim_ref, are_ref, aim_ref, cre_ref, cim_ref,
               yf_ref, yb_ref, fr_ref, fi_ref, br_ref, bi_ref, car_ref):
    q = uf_ref.shape[1]

    @pl.when(pl.program_id(1) == 0)
    def _():
        car_ref[...] = jnp.zeros_like(car_ref)

    for d, (u_ref, sr_ref, si_ref) in enumerate(((uf_ref, fr_ref, fi_ref), (ub_ref, br_ref, bi_ref))):
        for k in range(S5_SLABS):
            uk = u_ref[0, :, k * LANES:(k + 1) * LANES].astype(BF16)
            cols = slice(k * S5_SLAB_STATES, (k + 1) * S5_SLAB_STATES)
            sr_ref[:, cols] = _dot(uk, bre_ref[d, k])
            si_ref[:, cols] = _dot(uk, bim_ref[d, k])

    far, fai = are_ref[0], aim_ref[0]
    bar, bai = are_ref[1], aim_ref[1]

    def step(t, carry):
        sfr, sfi, sbr, sbi = carry
        rowf = pl.ds(t, 1)
        nfr = far * sfr - fai * sfi + fr_ref[rowf, :]
        nfi = far * sfi + fai * sfr + fi_ref[rowf, :]
        fr_ref[rowf, :] = nfr
        fi_ref[rowf, :] = nfi
        rowb = pl.ds(q - 1 - t, 1)
        nbr = bar * sbr - bai * sbi + br_ref[rowb, :]
        nbi = bar * sbi + bai * sbr + bi_ref[rowb, :]
        br_ref[rowb, :] = nbr
        bi_ref[rowb, :] = nbi
        return nfr, nfi, nbr, nbi

    init = (car_ref[0:1, :], car_ref[1:2, :], car_ref[2:3, :], car_ref[3:4, :])
    sfr, sfi, sbr, sbi = lax.fori_loop(0, q, step, init, unroll=8)
    car_ref[0:1, :] = sfr
    car_ref[1:2, :] = sfi
    car_ref[2:3, :] = sbr
    car_ref[3:4, :] = sbi

    for d, (y_ref, sr_ref, si_ref) in enumerate(((yf_ref, fr_ref, fi_ref), (yb_ref, br_ref, bi_ref))):
        for k in range(S5_SLABS):
            cols = slice(k * S5_SLAB_STATES, (k + 1) * S5_SLAB_STATES)
            y_ref[0, :, k * LANES:(k + 1) * LANES] = (
                _dot(sr_ref[:, cols].astype(BF16), cre_ref[d, k])
                + _dot(si_ref[:, cols].astype(BF16), cim_ref[d, k]))


def _s5_scan(u, s5p):
    b, l, _ = u.shape
    q = min(S5_CHUNK, l)
    nc = l // q
    fwd = pl.BlockSpec((1, q, S5_WIDTH), lambda i, c: (i, c, 0))
    bwd = pl.BlockSpec((1, q, S5_WIDTH), lambda i, c: (i, nc - 1 - c, 0))
    weights = (s5p["bre"], s5p["bim"], s5p["are"], s5p["aim"], s5p["cre"], s5p["cim"])
    return pl.pallas_call(
        _s5_kernel,
        grid=(b, nc),
        in_specs=[fwd, bwd] + [_resident(w.shape) for w in weights],
        out_specs=[fwd, bwd],
        out_shape=[jax.ShapeDtypeStruct(u.shape, F32)] * 2,
        scratch_shapes=[pltpu.VMEM((q, S5_LANES), F32)] * 4 + [pltpu.VMEM((SUBLANES, S5_LANES), F32)],
        compiler_params=_cparams(),
        name="s5scan",
    )(u, u, *weights)


def _conv_kernel(x_ref, prev_ref, next_ref, w_ref, b_ref, o_ref):
    j = pl.program_id(1)
    nj = pl.num_programs(1)
    x = x_ref[0]
    tq = x.shape[0]
    prev_row = jnp.where(j == 0, 0.0, prev_ref[0, SUBLANES - 1:SUBLANES, :])
    next0 = jnp.where(j == nj - 1, 0.0, next_ref[0, 0:1, :])
    next1 = jnp.where(j == nj - 1, 0.0, next_ref[0, 1:2, :])
    r = lax.broadcasted_iota(jnp.int32, x.shape, 0)
    xm1 = jnp.where(r == 0, prev_row, pltpu.roll(x, 1, 0))
    xp1 = jnp.where(r == tq - 1, next0, pltpu.roll(x, tq - 1, 0))
    xp2 = jnp.where(r == tq - 1, next1, jnp.where(r == tq - 2, next0, pltpu.roll(x, tq - 2, 0)))
    w = w_ref[...]
    acc = w[0:1] * xm1 + w[1:2] * x + w[2:3] * xp1 + w[3:4] * xp2 + b_ref[...]
    o_ref[0] = _silu(acc)


def _conv_silu(xbc, conv_w, conv_b):
    b, l, c = xbc.shape
    tq = min(CONV_TILE, l)
    nj = l // tq
    rows8 = tq // SUBLANES
    cur = pl.BlockSpec((1, tq, c), lambda i, j: (i, j, 0))
    prev = pl.BlockSpec((1, SUBLANES, c), lambda i, j: (i, jnp.maximum(j * rows8 - 1, 0), 0))
    nxt = pl.BlockSpec((1, SUBLANES, c), lambda i, j: (i, jnp.minimum((j + 1) * rows8, l // SUBLANES - 1), 0))
    return pl.pallas_call(
        _conv_kernel,
        grid=(b, nj),
        in_specs=[cur, prev, nxt, _resident(conv_w.shape), _resident(conv_b.shape)],
        out_specs=cur,
        out_shape=jax.ShapeDtypeStruct(xbc.shape, F32),
        compiler_params=_cparams(),
        name="conv",
    )(xbc, xbc, xbc, conv_w, conv_b)


def _ssd_chunk(xa, dtr_tl, dtr_hl, dtb_tl, alog_tl, dtb_hl, alog_hl, st_ref, d, reverse):
    q = xa.shape[0]
    hs = slice(d * M2_HEADS, (d + 1) * M2_HEADS)
    dt_tl = _softplus(dtr_tl + dtb_tl)
    da_tl = dt_tl * (-jnp.exp(alog_tl))
    dt_hl = _softplus(dtr_hl[hs, :] + dtb_hl[hs, :])
    da_hl = dt_hl * (-jnp.exp(alog_hl[hs, :]))

    r = lax.broadcasted_iota(jnp.int32, (q, q), 0)
    c = lax.broadcasted_iota(jnp.int32, (q, q), 1)
    low = jnp.where(c <= r, 1.0, 0.0).astype(BF16)
    upp = jnp.where(c >= r, 1.0, 0.0).astype(BF16)
    left, right = (upp, low) if reverse else (low, upp)
    pcs_tl = sum(_dot(left, p) for p in _split3(da_tl))
    pcs_hl = sum(_dot(p, right) for p in _split3(da_hl))
    end = 0 if reverse else q - 1
    tot_hl = pcs_hl[:, end:end + 1]
    keep = (c >= r) if reverse else (c <= r)

    lane = lax.broadcasted_iota(jnp.int32, (q, LANES), 1)
    first_head = lane < M2_HEADDIM
    lane_s = lax.broadcasted_iota(jnp.int32, (M2_STATE, LANES), 1)
    first_head_s = lane_s < M2_HEADDIM

    ys = []
    for g in range(M2_GROUPS):
        bm = xa[:, M2_INNER + g * M2_STATE:M2_INNER + (g + 1) * M2_STATE]
        cm = xa[:, M2_INNER + M2_GN + g * M2_STATE:M2_INNER + M2_GN + (g + 1) * M2_STATE].astype(BF16)
        cb = lax.dot_general(cm, bm.astype(BF16), (((1,), (1,)), ((), ())), preferred_element_type=F32)
        bt = bm.T
        st_g = st_ref[d, g]
        yoff = _dot(cm, st_g.astype(BF16))
        new_cols = []
        for jp in range(HEADS_PER_GROUP // 2):
            h0 = g * HEADS_PER_GROUP + 2 * jp
            pair = slice(h0 * M2_HEADDIM, (h0 + 2) * M2_HEADDIM)
            xp = xa[:, pair]
            xbd = jnp.concatenate([jnp.where(first_head, xp, 0.0), jnp.where(first_head, 0.0, xp)],
                                  axis=0).astype(BF16)
            ms, ws, ecols, decs = [], [], [], []
            for h in (h0, h0 + 1):
                col = pcs_tl[:, d * M2_HEADS + h:d * M2_HEADS + h + 1]
                row = pcs_hl[h:h + 1, :]
                dtrow = dt_hl[h:h + 1, :]
                decay = jnp.where(keep, jnp.exp(col - row), 0.0)
                ms.append((cb * decay * dtrow).astype(BF16))
                ws.append((bt * (jnp.exp(tot_hl[h:h + 1, :] - row) * dtrow)).astype(BF16))
                ecols.append(jnp.exp(col))
                decs.append(jnp.exp(tot_hl[h:h + 1, :]))
            m2 = jnp.concatenate(ms, axis=1)
            w2 = jnp.concatenate(ws, axis=1)
            yslab = yoff[:, 2 * jp * M2_HEADDIM:(2 * jp + 2) * M2_HEADDIM]
            ys.append(_dot(m2, xbd) + jnp.where(first_head, ecols[0], ecols[1]) * yslab)
            sslab = st_g[:, 2 * jp * M2_HEADDIM:(2 * jp + 2) * M2_HEADDIM]
            new_cols.append(jnp.where(first_head_s, decs[0], decs[1]) * sslab + _dot(w2, xbd))
        st_ref[d, g] = jnp.concatenate(new_cols, axis=1)
    return jnp.concatenate(ys, axis=1)


def _ssd_kernel(xf_ref, xb_ref, dtf_ref, dtb_ref, dthf_ref, dthb_ref,
                dtb_tl_ref, alog_tl_ref, dtb_hl_ref, alog_hl_ref, yf_ref, yb_ref, st_ref):
    @pl.when(pl.program_id(1) == 0)
    def _():
        st_ref[...] = jnp.zeros_like(st_ref)

    consts = (dtb_tl_ref[...], alog_tl_ref[...], dtb_hl_ref[...], alog_hl_ref[...])
    yf_ref[0] = _ssd_chunk(xf_ref[0], dtf_ref[0], dthf_ref[0], *consts, st_ref, 0, False)
    yb_ref[0] = _ssd_chunk(xb_ref[0], dtb_ref[0], dthb_ref[0], *consts, st_ref, 1, True)


def _ssd(xact, dt_tl, dt_hl, m2p):
    b, l, _ = xact.shape
    q = min(SSD_CHUNK, l)
    nc = l // q
    fwd3 = lambda n: pl.BlockSpec((1, q, n), lambda i, c: (i, c, 0))
    bwd3 = lambda n: pl.BlockSpec((1, q, n), lambda i, c: (i, nc - 1 - c, 0))
    hl_f = pl.BlockSpec((1, N_DIR * M2_HEADS, q), lambda i, c: (i, 0, c))
    hl_b = pl.BlockSpec((1, N_DIR * M2_HEADS, q), lambda i, c: (i, 0, nc - 1 - c))
    consts = (m2p["dtb_tl"], m2p["alog_tl"], m2p["dtb_hl"], m2p["alog_hl"])
    return pl.pallas_call(
        _ssd_kernel,
        grid=(b, nc),
        in_specs=[fwd3(M2_CONV_DIM), bwd3(M2_CONV_DIM), fwd3(DT_PAD), bwd3(DT_PAD), hl_f, hl_b]
        + [_resident(w.shape) for w in consts],
        out_specs=[fwd3(M2_INNER), bwd3(M2_INNER)],
        out_shape=[jax.ShapeDtypeStruct((b, l, M2_INNER), F32)] * 2,
        scratch_shapes=[pltpu.VMEM((N_DIR, M2_GROUPS, M2_STATE, HEADS_PER_GROUP * M2_HEADDIM), F32)],
        compiler_params=_cparams(),
        name="ssd",
    )(xact, xact, dt_tl, dt_tl, dt_hl, dt_hl, *consts)


def _mix_kernel(x_ref, g_ref, u_ref, s5f_ref, s5b_ref, xs_ref, mf_ref, mb_ref, z_ref,
                ds5_ref, wglu_ref, bglu_ref, ws5_ref, dm2_ref, nm2_ref, wm2_ref, wo_ref, o_ref):
    y = ds5_ref[...] * u_ref[...] + s5f_ref[...] + s5b_ref[...]
    h = jax.nn.gelu(y)
    h = h * _sigmoid(_dot(h.astype(BF16), wglu_ref[...]) + bglu_ref[...])
    s5 = _dot(h.astype(BF16), ws5_ref[...])
    ym = dm2_ref[...] * xs_ref[...] + mf_ref[...] + mb_ref[...]
    gz = ym * _silu(z_ref[...])
    gw = M2_INNER // M2_GROUPS
    parts = []
    for i in range(M2_GROUPS):
        gs = gz[:, i * gw:(i + 1) * gw]
        parts.append(gs * lax.rsqrt(jnp.mean(gs * gs, axis=-1, keepdims=True) + EPS))
    gn = jnp.concatenate(parts, axis=1) * nm2_ref[...]
    m2 = _dot(gn.astype(BF16), wm2_ref[...])
    gates = _sigmoid(g_ref[...])
    merged = gates[:, :D_MODEL] * s5 + gates[:, D_MODEL:] * m2
    o_ref[...] = x_ref[...] + _dot(merged.astype(BF16), wo_ref[...])


def _mix(x2d, gates, u, s5f, s5b, xact, mf, mb, z, mp):
    t = x2d.shape[0]
    tm = TM_MIX
    row = lambda n: pl.BlockSpec((tm, n), lambda i: (i, 0))
    weights = (mp["d_s5"], mp["w_glu"], mp["b_glu"], mp["w_s5_out"], mp["d_m2"], mp["m2_norm_w"],
               mp["w_m2_out"], mp["w_o"])
    return pl.pallas_call(
        _mix_kernel,
        grid=(t // tm,),
        in_specs=[row(D_MODEL), row(2 * D_MODEL), row(S5_WIDTH), row(S5_WIDTH), row(S5_WIDTH),
                  row(M2_INNER), row(M2_INNER), row(M2_INNER), row(M2_INNER)]
        + [_resident(w.shape) for w in weights],
        out_specs=row(D_MODEL),
        out_shape=jax.ShapeDtypeStruct((t, D_MODEL), F32),
        compiler_params=_cparams(),
        name="mix",
    )(x2d, gates, u, s5f, s5b, xact, mf, mb, z, *weights)


def _mlp_kernel(x_ref, n2_ref, wup_ref, wdn_ref, fn_ref, o_ref, *, final):
    x = x_ref[...]
    h = _rms(x, n2_ref[...]).astype(BF16)
    acc = x
    for c0 in range(0, D_FF, 2 * N_CHUNK_COLS):
        c1 = c0 + 2 * N_CHUNK_COLS
        hid = jnp.maximum(_dot(h, wup_ref[:, c0:c1]), 0.0)
        acc = acc + _dot((hid * hid).astype(BF16), wdn_ref[c0:c1, :])
    o_ref[...] = _rms(acc, fn_ref[...]) if final else acc


def _mlp(x2d, norm2_w, w_up, w_down, final_norm_w, final):
    t = x2d.shape[0]
    tm = TM_MLP
    row = pl.BlockSpec((tm, D_MODEL), lambda i: (i, 0))
    weights = (norm2_w, w_up, w_down, final_norm_w)
    return pl.pallas_call(
        functools.partial(_mlp_kernel, final=final),
        grid=(t // tm,),
        in_specs=[row] + [_resident(w.shape) for w in weights],
        out_specs=row,
        out_shape=jax.ShapeDtypeStruct((t, D_MODEL), F32),
        compiler_params=_cparams(),
        name="mlp",
    )(x2d, *weights)


def _block_diag_slabs(m):
    nd, _, a, b = m.shape
    m = m.reshape(nd, S5_SLABS, S5_SLAB_GROUPS, a, b)
    eye = jnp.eye(S5_SLAB_GROUPS, dtype=m.dtype)
    out = m[:, :, :, :, None, :] * eye[None, None, :, None, :, None]
    return out.reshape(nd, S5_SLABS, S5_SLAB_GROUPS * a, S5_SLAB_GROUPS * b)


def _s5_params(lam_re, lam_im, log_dt, b_re, b_im, c_re, c_im):
    delta = jnp.exp(log_dt)[..., None]
    mag = jnp.exp(lam_re * delta)
    a_re = mag * jnp.cos(lam_im * delta)
    a_im = mag * jnp.sin(lam_im * delta)
    inv = 1.0 / (lam_re * lam_re + lam_im * lam_im)
    q_re = ((a_re - 1.0) * lam_re + a_im * lam_im) * inv
    q_im = (a_im * lam_re - (a_re - 1.0) * lam_im) * inv
    bbar_re = q_re[..., None] * b_re - q_im[..., None] * b_im
    bbar_im = q_re[..., None] * b_im + q_im[..., None] * b_re
    to_in = lambda m: _block_diag_slabs(jnp.swapaxes(m, -1, -2)).astype(BF16)
    to_out = lambda m: _block_diag_slabs(jnp.swapaxes(m, -1, -2)).astype(BF16)
    return {
        "bre": to_in(bbar_re), "bim": to_in(bbar_im),
        "are": a_re.reshape(N_DIR, 1, S5_LANES), "aim": a_im.reshape(N_DIR, 1, S5_LANES),
        "cre": to_out(c_re), "cim": to_out(-c_im),
    }


def _m2_params(dt_bias, a_log):
    pad = DT_PAD - N_DIR * M2_HEADS
    flat = lambda v: v.reshape(1, N_DIR * M2_HEADS)
    return {
        "dtb_tl": jnp.pad(flat(dt_bias), ((0, 0), (0, pad))),
        "alog_tl": jnp.pad(flat(a_log), ((0, 0), (0, pad))),
        "dtb_hl": flat(dt_bias).T,
        "alog_hl": flat(a_log).T,
    }


def _pad_w_in(w_in):
    o_z = S5_WIDTH
    o_xbc = o_z + M2_INNER
    o_dt = o_xbc + M2_CONV_DIM
    o_g = o_dt + N_DIR * M2_HEADS
    dt = jnp.pad(w_in[:, o_dt:o_g], ((0, 0), (0, DT_PAD - N_DIR * M2_HEADS)))
    return jnp.concatenate([w_in[:, :o_dt], dt, w_in[:, o_g:]], axis=1).astype(BF16)


def _layer(x2d, b, l, lp, final_norm_w, final):
    u, z, xbc, dt, gates = _inproj(x2d, lp["norm1_w"], lp["w_in"])
    s5f, s5b = _s5_scan(u.reshape(b, l, S5_WIDTH), lp["s5"])
    xact = _conv_silu(xbc.reshape(b, l, M2_CONV_DIM), lp["conv_w"], lp["conv_b"])
    dt3 = dt.reshape(b, l, DT_PAD)
    dt_hl = jnp.swapaxes(dt3[:, :, :N_DIR * M2_HEADS], 1, 2)
    mf, mb = _ssd(xact, dt3, dt_hl, lp["m2"])
    t = b * l
    x1 = _mix(x2d, gates, u, s5f.reshape(t, -1), s5b.reshape(t, -1), xact.reshape(t, -1),
              mf.reshape(t, -1), mb.reshape(t, -1), z, lp)
    return _mlp(x1, lp["norm2_w"], lp["w_up"], lp["w_down"], final_norm_w, final)


def _trunk(x, layers, final_norm_w):
    b, l, _ = x.shape
    x2d = x.reshape(b * l, D_MODEL)
    for i, lp in enumerate(layers):
        x2d = _layer(x2d, b, l, lp, final_norm_w, i == len(layers) - 1)
    return x2d.reshape(b, l, D_MODEL)


def _prepare_layers(norm1_w, w_in, lam_re, lam_im, log_dt, b_re, b_im, c_re, c_im, d_s5, w_glu,
                    b_glu, w_s5_out, conv_w, conv_b, dt_bias, a_log, d_m2, m2_norm_w, w_m2_out,
                    w_o, norm2_w, w_up, w_down):
    layers = []
    for i in range(norm1_w.shape[0]):
        layers.append({
            "norm1_w": norm1_w[i].reshape(1, -1),
            "w_in": _pad_w_in(w_in[i]),
            "s5": _s5_params(lam_re[i], lam_im[i], log_dt[i], b_re[i], b_im[i], c_re[i], c_im[i]),
            "d_s5": d_s5[i].reshape(1, -1),
            "w_glu": w_glu[i].astype(BF16),
            "b_glu": b_glu[i].reshape(1, -1),
            "w_s5_out": w_s5_out[i].astype(BF16),
            "conv_w": conv_w[i],
            "conv_b": conv_b[i].reshape(1, -1),
            "m2": _m2_params(dt_bias[i], a_log[i]),
            "d_m2": jnp.repeat(d_m2[i], M2_HEADDIM).reshape(1, -1),
            "m2_norm_w": m2_norm_w[i].reshape(1, -1),
            "w_m2_out": w_m2_out[i].astype(BF16),
            "w_o": w_o[i].astype(BF16),
            "norm2_w": norm2_w[i].reshape(1, -1),
            "w_up": w_up[i].astype(BF16),
            "w_down": w_down[i].astype(BF16),
        })
    return layers


def kernel(x_prompt, x_sample, norm1_w, w_in, lam_re, lam_im, log_dt, b_re, b_im, c_re, c_im, d_s5, w_glu, b_glu, w_s5_out, conv_w, conv_b, dt_bias, a_log, d_m2, m2_norm_w, w_m2_out, w_o, norm2_w, w_up, w_down, final_norm_w):
    layers = _prepare_layers(norm1_w, w_in, lam_re, lam_im, log_dt, b_re, b_im, c_re, c_im, d_s5,
                             w_glu, b_glu, w_s5_out, conv_w, conv_b, dt_bias, a_log, d_m2,
                             m2_norm_w, w_m2_out, w_o, norm2_w, w_up, w_down)
    fnw = final_norm_w.reshape(1, -1)
    return (_trunk(x_prompt, layers, fnw), _trunk(x_sample, layers, fnw))
```

```python
import functools

import jax
import jax.numpy as jnp
from jax import lax
from jax.experimental import pallas as pl
from jax.experimental.pallas import tpu as pltpu

F32 = jnp.float32
BF16 = jnp.bfloat16

D_MODEL = 1024
DEPTH = 2
N_DIR = 2
EPS = 1e-6
S5_WIDTH = 768
S5_GROUP = 16
S5_GROUPS = 48
S5_STATE = 64
S5_LANES = S5_GROUPS * S5_STATE
M2_INNER = 1536
M2_HEADDIM = 64
M2_HEADS = 24
M2_GROUPS = 4
M2_STATE = 128
M2_CONV = 4
M2_GN = M2_GROUPS * M2_STATE
M2_CONV_DIM = M2_INNER + 2 * M2_GN
HEADS_PER_GROUP = M2_HEADS // M2_GROUPS
D_FF = 4096

LANES = 128
SUBLANES = 8
VMEM_LIMIT_BYTES = 56 * 1024 * 1024

HALO_ROWS = 16
DT_PAD = LANES
S5_SLAB_GROUPS = LANES // S5_GROUP
S5_SLABS = S5_WIDTH // LANES
S5_SLAB_STATES = S5_SLAB_GROUPS * S5_STATE

TM_INPROJ = 256
TM_MIX = 256
TM_MLP = 512
S5_CHUNK = 256
SSD_CHUNK = 128
CONV_TILE = 512
CONV_COLS = 256
CONV_SUB = 128
N_CHUNK_COLS = 512


def _cparams():
    return pltpu.CompilerParams(dimension_semantics=None, vmem_limit_bytes=VMEM_LIMIT_BYTES)


def _resident(shape):
    nd = len(shape)
    return pl.BlockSpec(shape, lambda *_: (0,) * nd, pipeline_mode=pl.Buffered(1))


def _silu(x):
    return x * (1.0 / (1.0 + jnp.exp(-x)))


def _sigmoid(x):
    return 1.0 / (1.0 + jnp.exp(-x))


def _softplus(x):
    return jnp.maximum(x, 0.0) + jnp.log1p(jnp.exp(-jnp.abs(x)))


def _rms(x, w):
    var = jnp.mean(x * x, axis=-1, keepdims=True)
    return x * lax.rsqrt(var + EPS) * w


def _dot(a, b):
    return jnp.dot(a, b, preferred_element_type=F32)


def _split3(x):
    hi = x.astype(BF16)
    r1 = x - hi.astype(F32)
    mid = r1.astype(BF16)
    lo = (r1 - mid.astype(F32)).astype(BF16)
    return hi, mid, lo


def _inproj_kernel(x_ref, nw_ref, w_ref, dtb_ref, u_ref, z_ref, xbc_ref, dt_ref, g_ref):
    h = _rms(x_ref[...], nw_ref[...]).astype(BF16)
    ident = lambda v: v
    posts = (ident, ident, ident, lambda v: _softplus(v + dtb_ref[...]), _sigmoid)
    off = 0
    for ref, post in zip((u_ref, z_ref, xbc_ref, dt_ref, g_ref), posts):
        n = ref.shape[-1]
        for c0 in range(0, n, N_CHUNK_COLS):
            c1 = min(c0 + N_CHUNK_COLS, n)
            ref[:, c0:c1] = post(_dot(h, w_ref[:, off + c0:off + c1])).astype(ref.dtype)
        off += n


def _inproj(x2d, norm_w, w_in_p, dtb_tl):
    t = x2d.shape[0]
    tm = TM_INPROJ
    widths = (S5_WIDTH, M2_INNER, M2_CONV_DIM, DT_PAD, 2 * D_MODEL)
    dtypes = (BF16, BF16, BF16, F32, BF16)
    row = lambda n: pl.BlockSpec((tm, n), lambda i: (i, 0))
    return pl.pallas_call(
        _inproj_kernel,
        grid=(t // tm,),
        in_specs=[row(D_MODEL), _resident((1, D_MODEL)), _resident(w_in_p.shape), _resident(dtb_tl.shape)],
        out_specs=[row(n) for n in widths],
        out_shape=[jax.ShapeDtypeStruct((t, n), dt) for n, dt in zip(widths, dtypes)],
        compiler_params=_cparams(),
        name="inproj",
    )(x2d, norm_w, w_in_p, dtb_tl)


def _s5_kernel(uf_ref, ub_ref, bre_ref, bim_ref, are_ref, aim_ref, cre_ref, cim_ref,
               yf_ref, yb_ref, fr_ref, fi_ref, br_ref, bi_ref, car_ref):
    q = uf_ref.shape[1]
    sfr_ref, sfi_ref, sbr_ref, sbi_ref = fr_ref, fi_ref, br_ref, bi_ref

    @pl.when(pl.program_id(1) == 0)
    def _():
        car_ref[...] = jnp.zeros_like(car_ref)

    for d, (u_ref, sr_ref, si_ref) in enumerate(((uf_ref, fr_ref, fi_ref), (ub_ref, br_ref, bi_ref))):
        for k in range(S5_SLABS):
            uk = u_ref[0, :, k * LANES:(k + 1) * LANES]
            cols = slice(k * S5_SLAB_STATES, (k + 1) * S5_SLAB_STATES)
            sr_ref[:, cols] = _dot(uk, bre_ref[d, k])
            si_ref[:, cols] = _dot(uk, bim_ref[d, k])

    far, fai = are_ref[0], aim_ref[0]
    bar, bai = are_ref[1], aim_ref[1]

    def step(t, carry):
        sfr, sfi, sbr, sbi = carry
        rowf = pl.ds(t, 1)
        nfr = far * sfr - fai * sfi + fr_ref[rowf, :]
        nfi = far * sfi + fai * sfr + fi_ref[rowf, :]
        sfr_ref[rowf, :] = nfr
        sfi_ref[rowf, :] = nfi
        rowb = pl.ds(q - 1 - t, 1)
        nbr = bar * sbr - bai * sbi + br_ref[rowb, :]
        nbi = bar * sbi + bai * sbr + bi_ref[rowb, :]
        sbr_ref[rowb, :] = nbr
        sbi_ref[rowb, :] = nbi
        return nfr, nfi, nbr, nbi

    init = (car_ref[0:1, :], car_ref[1:2, :], car_ref[2:3, :], car_ref[3:4, :])
    sfr, sfi, sbr, sbi = lax.fori_loop(0, q, step, init, unroll=8)
    car_ref[0:1, :] = sfr
    car_ref[1:2, :] = sfi
    car_ref[2:3, :] = sbr
    car_ref[3:4, :] = sbi

    for d, (y_ref, sr_ref, si_ref) in enumerate(((yf_ref, sfr_ref, sfi_ref), (yb_ref, sbr_ref, sbi_ref))):
        for k in range(S5_SLABS):
            cols = slice(k * S5_SLAB_STATES, (k + 1) * S5_SLAB_STATES)
            y_ref[0, :, k * LANES:(k + 1) * LANES] = (
                _dot(sr_ref[:, cols].astype(BF16), cre_ref[d, k])
                + _dot(si_ref[:, cols].astype(BF16), cim_ref[d, k])).astype(y_ref.dtype)


def _s5_scan(u, s5p):
    b, l, _ = u.shape
    q = min(S5_CHUNK, l)
    nc = l // q
    fwd = pl.BlockSpec((1, q, S5_WIDTH), lambda i, c: (i, c, 0))
    bwd = pl.BlockSpec((1, q, S5_WIDTH), lambda i, c: (i, nc - 1 - c, 0))
    weights = (s5p["bre"], s5p["bim"], s5p["are"], s5p["aim"], s5p["cre"], s5p["cim"])
    return pl.pallas_call(
        _s5_kernel,
        grid=(b, nc),
        in_specs=[fwd, bwd] + [_resident(w.shape) for w in weights],
        out_specs=[fwd, bwd],
        out_shape=[jax.ShapeDtypeStruct(u.shape, BF16)] * 2,
        scratch_shapes=[pltpu.VMEM((q, S5_LANES), F32)] * 4 + [pltpu.VMEM((SUBLANES, S5_LANES), F32)],
        compiler_params=_cparams(),
        name="s5scan",
    )(u, u, *weights)


def _conv_kernel(x_ref, prev_ref, next_ref, w_ref, b_ref, o_ref, buf_ref):
    j = pl.program_id(1)
    nj = pl.num_programs(1)
    tq = x_ref.shape[1]
    pad = HALO_ROWS
    buf_ref[pad:pad + tq, :] = x_ref[0]
    buf_ref[0:pad, :] = jnp.where(j == 0, jnp.zeros_like(prev_ref[0]), prev_ref[0])
    buf_ref[pad + tq:2 * pad + tq, :] = jnp.where(j == nj - 1, jnp.zeros_like(next_ref[0]), next_ref[0])
    sub = CONV_SUB
    r = lax.broadcasted_iota(jnp.int32, (sub, sub + 2 * pad), 0)
    c = lax.broadcasted_iota(jnp.int32, (sub, sub + 2 * pad), 1)
    shifts = {k: jnp.where(c == r + pad + k - 1, 1.0, 0.0).astype(BF16) for k in (0, 2, 3)}
    w = w_ref[...]
    for r0 in range(0, tq, sub):
        for c0 in range(0, x_ref.shape[2], CONV_COLS):
            cs = slice(c0, c0 + CONV_COLS)
            ext = buf_ref[r0:r0 + sub + 2 * pad, cs]
            acc = b_ref[:, cs] + w[1:2, cs] * ext[pad:pad + sub].astype(F32)
            for k in (0, 2, 3):
                acc = acc + w[k:k + 1, cs] * _dot(shifts[k], ext)
            o_ref[0, r0:r0 + sub, cs] = _silu(acc).astype(o_ref.dtype)


def _conv_silu(xbc, conv_w, conv_b):
    b, l, c = xbc.shape
    tq = min(CONV_TILE, l)
    nj = l // tq
    nh = tq // HALO_ROWS
    cur = pl.BlockSpec((1, tq, c), lambda i, j: (i, j, 0))
    prev = pl.BlockSpec((1, HALO_ROWS, c), lambda i, j: (i, jnp.maximum(j * nh - 1, 0), 0))
    nxt = pl.BlockSpec((1, HALO_ROWS, c), lambda i, j: (i, jnp.minimum((j + 1) * nh, l // HALO_ROWS - 1), 0))
    return pl.pallas_call(
        _conv_kernel,
        grid=(b, nj),
        in_specs=[cur, prev, nxt, _resident(conv_w.shape), _resident(conv_b.shape)],
        out_specs=cur,
        out_shape=jax.ShapeDtypeStruct(xbc.shape, BF16),
        scratch_shapes=[pltpu.VMEM((tq + 2 * HALO_ROWS, c), BF16)],
        compiler_params=_cparams(),
        name="conv",
    )(xbc, xbc, xbc, conv_w, conv_b)


def _ssd_chunk(xa, dt_tl, dt_hl2, alog_tl, alog_hl, st_ref, d, reverse):
    q = xa.shape[0]
    hs = slice(d * M2_HEADS, (d + 1) * M2_HEADS)
    da_tl = dt_tl * (-jnp.exp(alog_tl))
    dt_hl = dt_hl2[hs, :]
    da_hl = dt_hl * (-jnp.exp(alog_hl[hs, :]))

    r = lax.broadcasted_iota(jnp.int32, (q, q), 0)
    c = lax.broadcasted_iota(jnp.int32, (q, q), 1)
    low = jnp.where(c <= r, 1.0, 0.0).astype(BF16)
    upp = jnp.where(c >= r, 1.0, 0.0).astype(BF16)
    left, right = (upp, low) if reverse else (low, upp)
    pcs_tl = sum(_dot(left, p) for p in _split3(da_tl))
    pcs_hl = sum(_dot(p, right) for p in _split3(da_hl))
    end = 0 if reverse else q - 1
    tot_hl = pcs_hl[:, end:end + 1]
    keep = (c >= r) if reverse else (c <= r)

    lane = lax.broadcasted_iota(jnp.int32, (q, LANES), 1)
    first_head = lane < M2_HEADDIM
    lane_s = lax.broadcasted_iota(jnp.int32, (M2_STATE, LANES), 1)
    first_head_s = lane_s < M2_HEADDIM

    ys = []
    for g in range(M2_GROUPS):
        bm = xa[:, M2_INNER + g * M2_STATE:M2_INNER + (g + 1) * M2_STATE]
        cm = xa[:, M2_INNER + M2_GN + g * M2_STATE:M2_INNER + M2_GN + (g + 1) * M2_STATE]
        cb = lax.dot_general(cm, bm, (((1,), (1,)), ((), ())), preferred_element_type=F32)
        bt = bm.astype(F32).T
        st_g = st_ref[d, g]
        yoff = _dot(cm, st_g.astype(BF16))
        new_cols = []
        for jp in range(HEADS_PER_GROUP // 2):
            h0 = g * HEADS_PER_GROUP + 2 * jp
            pair = slice(h0 * M2_HEADDIM, (h0 + 2) * M2_HEADDIM)
            xp = xa[:, pair]
            xpf = xp.astype(F32)
            xbd = jnp.concatenate([jnp.where(first_head, xpf, 0.0), jnp.where(first_head, 0.0, xpf)],
                                  axis=0).astype(BF16)
            ms, ws, ecols, decs = [], [], [], []
            for h in (h0, h0 + 1):
                col = jnp.broadcast_to(pcs_tl[:, d * M2_HEADS + h:d * M2_HEADS + h + 1], (q, LANES))
                row = pcs_hl[h:h + 1, :]
                dtrow = dt_hl[h:h + 1, :]
                decay = jnp.where(keep, jnp.exp(jnp.tile(col, (1, q // LANES)) - row), 0.0)
                ms.append((cb * decay * dtrow).astype(BF16))
                ws.append((bt * (jnp.exp(tot_hl[h:h + 1, :] - row) * dtrow)).astype(BF16))
                ecols.append(jnp.exp(col))
                decs.append(jnp.exp(tot_hl[h:h + 1, :]))
            m2 = jnp.concatenate(ms, axis=1)
            w2 = jnp.concatenate(ws, axis=1)
            yslab = yoff[:, 2 * jp * M2_HEADDIM:(2 * jp + 2) * M2_HEADDIM]
            ys.append(_dot(m2, xbd) + jnp.where(first_head, ecols[0], ecols[1]) * yslab)
            sslab = st_g[:, 2 * jp * M2_HEADDIM:(2 * jp + 2) * M2_HEADDIM]
            new_cols.append(jnp.where(first_head_s, decs[0], decs[1]) * sslab + _dot(w2, xbd))
        st_ref[d, g] = jnp.concatenate(new_cols, axis=1)
    return jnp.concatenate(ys, axis=1)


def _ssd_kernel(xf_ref, xb_ref, dtf_ref, dtb_ref, dthf_ref, dthb_ref,
                alog_tl_ref, alog_hl_ref, yf_ref, yb_ref, st_ref):
    @pl.when(pl.program_id(1) == 0)
    def _():
        st_ref[...] = jnp.zeros_like(st_ref)

    consts = (alog_tl_ref[...], alog_hl_ref[...])
    yf_ref[0] = _ssd_chunk(xf_ref[0], dtf_ref[0], dthf_ref[0], *consts, st_ref, 0, False).astype(yf_ref.dtype)
    yb_ref[0] = _ssd_chunk(xb_ref[0], dtb_ref[0], dthb_ref[0], *consts, st_ref, 1, True).astype(yb_ref.dtype)


def _ssd(xact, dt_tl, dt_hl, m2p):
    b, l, _ = xact.shape
    q = min(SSD_CHUNK, l)
    nc = l // q
    fwd3 = lambda n: pl.BlockSpec((1, q, n), lambda i, c: (i, c, 0))
    bwd3 = lambda n: pl.BlockSpec((1, q, n), lambda i, c: (i, nc - 1 - c, 0))
    hl_f = pl.BlockSpec((1, N_DIR * M2_HEADS, q), lambda i, c: (i, 0, c))
    hl_b = pl.BlockSpec((1, N_DIR * M2_HEADS, q), lambda i, c: (i, 0, nc - 1 - c))
    consts = (m2p["alog_tl"], m2p["alog_hl"])
    return pl.pallas_call(
        _ssd_kernel,
        grid=(b, nc),
        in_specs=[fwd3(M2_CONV_DIM), bwd3(M2_CONV_DIM), fwd3(DT_PAD), bwd3(DT_PAD), hl_f, hl_b]
        + [_resident(w.shape) for w in consts],
        out_specs=[fwd3(M2_INNER), bwd3(M2_INNER)],
        out_shape=[jax.ShapeDtypeStruct((b, l, M2_INNER), BF16)] * 2,
        scratch_shapes=[pltpu.VMEM((N_DIR, M2_GROUPS, M2_STATE, HEADS_PER_GROUP * M2_HEADDIM), F32)],
        compiler_params=_cparams(),
        name="ssd",
    )(xact, xact, dt_tl, dt_tl, dt_hl, dt_hl, *consts)


def _mix_kernel(x_ref, g_ref, u_ref, s5f_ref, s5b_ref, xs_ref, mf_ref, mb_ref, z_ref,
                ds5_ref, wglu_ref, bglu_ref, ws5_ref, dm2_ref, nm2_ref, wm2_ref, wo_ref, o_ref):
    f = lambda ref: ref[...].astype(F32)
    y = ds5_ref[...] * f(u_ref) + f(s5f_ref) + f(s5b_ref)
    h = jax.nn.gelu(y)
    h = h * _sigmoid(_dot(h.astype(BF16), wglu_ref[...]) + bglu_ref[...])
    s5 = _dot(h.astype(BF16), ws5_ref[...])
    ym = dm2_ref[...] * f(xs_ref) + f(mf_ref) + f(mb_ref)
    gz = ym * _silu(f(z_ref))
    gw = M2_INNER // M2_GROUPS
    parts = []
    for i in range(M2_GROUPS):
        gs = gz[:, i * gw:(i + 1) * gw]
        parts.append(gs * lax.rsqrt(jnp.mean(gs * gs, axis=-1, keepdims=True) + EPS))
    gn = jnp.concatenate(parts, axis=1) * nm2_ref[...]
    m2 = _dot(gn.astype(BF16), wm2_ref[...])
    gates = f(g_ref)
    merged = gates[:, :D_MODEL] * s5 + gates[:, D_MODEL:] * m2
    o_ref[...] = x_ref[...] + _dot(merged.astype(BF16), wo_ref[...])


def _mix(x2d, gates, u, s5f, s5b, xact, mf, mb, z, mp):
    t = x2d.shape[0]
    tm = TM_MIX
    row = lambda n: pl.BlockSpec((tm, n), lambda i: (i, 0))
    weights = (mp["d_s5"], mp["w_glu"], mp["b_glu"], mp["w_s5_out"], mp["d_m2"], mp["m2_norm_w"],
               mp["w_m2_out"], mp["w_o"])
    return pl.pallas_call(
        _mix_kernel,
        grid=(t // tm,),
        in_specs=[row(D_MODEL), row(2 * D_MODEL), row(S5_WIDTH), row(S5_WIDTH), row(S5_WIDTH),
                  row(M2_INNER), row(M2_INNER), row(M2_INNER), row(M2_INNER)]
        + [_resident(w.shape) for w in weights],
        out_specs=row(D_MODEL),
        out_shape=jax.ShapeDtypeStruct((t, D_MODEL), F32),
        compiler_params=_cparams(),
        name="mix",
    )(x2d, gates, u, s5f, s5b, xact, mf, mb, z, *weights)


def _mlp_kernel(x_ref, n2_ref, wup_ref, wdn_ref, fn_ref, o_ref, *, final):
    x = x_ref[...]
    h = _rms(x, n2_ref[...]).astype(BF16)
    acc = x
    for c0 in range(0, D_FF, 2 * N_CHUNK_COLS):
        c1 = c0 + 2 * N_CHUNK_COLS
        hid = jnp.maximum(_dot(h, wup_ref[:, c0:c1]), 0.0)
        acc = acc + _dot((hid * hid).astype(BF16), wdn_ref[c0:c1, :])
    o_ref[...] = _rms(acc, fn_ref[...]) if final else acc


def _mlp(x2d, norm2_w, w_up, w_down, final_norm_w, final):
    t = x2d.shape[0]
    tm = TM_MLP
    row = pl.BlockSpec((tm, D_MODEL), lambda i: (i, 0))
    weights = (norm2_w, w_up, w_down, final_norm_w)
    return pl.pallas_call(
        functools.partial(_mlp_kernel, final=final),
        grid=(t // tm,),
        in_specs=[row] + [_resident(w.shape) for w in weights],
        out_specs=row,
        out_shape=jax.ShapeDtypeStruct((t, D_MODEL), F32),
        compiler_params=_cparams(),
        name="mlp",
    )(x2d, *weights)


def _block_diag_slabs(m):
    nd, _, a, b = m.shape
    m = m.reshape(nd, S5_SLABS, S5_SLAB_GROUPS, a, b)
    eye = jnp.eye(S5_SLAB_GROUPS, dtype=m.dtype)
    out = m[:, :, :, :, None, :] * eye[None, None, :, None, :, None]
    return out.reshape(nd, S5_SLABS, S5_SLAB_GROUPS * a, S5_SLAB_GROUPS * b)


def _s5_params(lam_re, lam_im, log_dt, b_re, b_im, c_re, c_im):
    delta = jnp.exp(log_dt)[..., None]
    mag = jnp.exp(lam_re * delta)
    a_re = mag * jnp.cos(lam_im * delta)
    a_im = mag * jnp.sin(lam_im * delta)
    inv = 1.0 / (lam_re * lam_re + lam_im * lam_im)
    q_re = ((a_re - 1.0) * lam_re + a_im * lam_im) * inv
    q_im = (a_im * lam_re - (a_re - 1.0) * lam_im) * inv
    bbar_re = q_re[..., None] * b_re - q_im[..., None] * b_im
    bbar_im = q_re[..., None] * b_im + q_im[..., None] * b_re
    to_in = lambda m: _block_diag_slabs(jnp.swapaxes(m, -1, -2)).astype(BF16)
    to_out = lambda m: _block_diag_slabs(jnp.swapaxes(m, -1, -2)).astype(BF16)
    return {
        "bre": to_in(bbar_re), "bim": to_in(bbar_im),
        "are": a_re.reshape(N_DIR, 1, S5_LANES), "aim": a_im.reshape(N_DIR, 1, S5_LANES),
        "cre": to_out(c_re), "cim": to_out(-c_im),
    }


def _m2_params(dt_bias, a_log):
    pad = DT_PAD - N_DIR * M2_HEADS
    flat = lambda v: v.reshape(1, N_DIR * M2_HEADS)
    return {
        "dtb_tl": jnp.pad(flat(dt_bias), ((0, 0), (0, pad))),
        "alog_tl": jnp.pad(flat(a_log), ((0, 0), (0, pad))),
        "alog_hl": flat(a_log).T,
    }


def _pad_w_in(w_in):
    o_z = S5_WIDTH
    o_xbc = o_z + M2_INNER
    o_dt = o_xbc + M2_CONV_DIM
    o_g = o_dt + N_DIR * M2_HEADS
    dt = jnp.pad(w_in[:, o_dt:o_g], ((0, 0), (0, DT_PAD - N_DIR * M2_HEADS)))
    return jnp.concatenate([w_in[:, :o_dt], dt, w_in[:, o_g:]], axis=1).astype(BF16)


def _layer(x2d, b, l, lp, final_norm_w, final):
    u, z, xbc, dt, gates = _inproj(x2d, lp["norm1_w"], lp["w_in"], lp["m2"]["dtb_tl"])
    s5f, s5b = _s5_scan(u.reshape(b, l, S5_WIDTH), lp["s5"])
    xact = _conv_silu(xbc.reshape(b, l, M2_CONV_DIM), lp["conv_w"], lp["conv_b"])
    dt3 = dt.reshape(b, l, DT_PAD)
    dt_hl = jnp.swapaxes(dt3[:, :, :N_DIR * M2_HEADS], 1, 2)
    mf, mb = _ssd(xact, dt3, dt_hl, lp["m2"])
    t = b * l
    x1 = _mix(x2d, gates, u, s5f.reshape(t, -1), s5b.reshape(t, -1), xact.reshape(t, -1),
              mf.reshape(t, -1), mb.reshape(t, -1), z, lp)
    return _mlp(x1, lp["norm2_w"], lp["w_up"], lp["w_down"], final_norm_w, final)


def _trunk(x, layers, final_norm_w):
    b, l, _ = x.shape
    x2d = x.reshape(b * l, D_MODEL)
    for i, lp in enumerate(layers):
        x2d = _layer(x2d, b, l, lp, final_norm_w, i == len(layers) - 1)
    return x2d.reshape(b, l, D_MODEL)


def _prepare_layers(norm1_w, w_in, lam_re, lam_im, log_dt, b_re, b_im, c_re, c_im, d_s5, w_glu,
                    b_glu, w_s5_out, conv_w, conv_b, dt_bias, a_log, d_m2, m2_norm_w, w_m2_out,
                    w_o, norm2_w, w_up, w_down):
    layers = []
    for i in range(norm1_w.shape[0]):
        layers.append({
            "norm1_w": norm1_w[i].reshape(1, -1),
            "w_in": _pad_w_in(w_in[i]),
            "s5": _s5_params(lam_re[i], lam_im[i], log_dt[i], b_re[i], b_im[i], c_re[i], c_im[i]),
            "d_s5": d_s5[i].reshape(1, -1),
            "w_glu": w_glu[i].astype(BF16),
            "b_glu": b_glu[i].reshape(1, -1),
            "w_s5_out": w_s5_out[i].astype(BF16),
            "conv_w": conv_w[i],
            "conv_b": conv_b[i].reshape(1, -1),
            "m2": _m2_params(dt_bias[i], a_log[i]),
            "d_m2": jnp.repeat(d_m2[i], M2_HEADDIM).reshape(1, -1),
            "m2_norm_w": m2_norm_w[i].reshape(1, -1),
            "w_m2_out": w_m2_out[i].astype(BF16),
            "w_o": w_o[i].astype(BF16),
            "norm2_w": norm2_w[i].reshape(1, -1),
            "w_up": w_up[i].astype(BF16),
            "w_down": w_down[i].astype(BF16),
        })
    return layers


def kernel(x_prompt, x_sample, norm1_w, w_in, lam_re, lam_im, log_dt, b_re, b_im, c_re, c_im, d_s5, w_glu, b_glu, w_s5_out, conv_w, conv_b, dt_bias, a_log, d_m2, m2_norm_w, w_m2_out, w_o, norm2_w, w_up, w_down, final_norm_w):
    layers = _prepare_layers(norm1_w, w_in, lam_re, lam_im, log_dt, b_re, b_im, c_re, c_im, d_s5,
                             w_glu, b_glu, w_s5_out, conv_w, conv_b, dt_bias, a_log, d_m2,
                             m2_norm_w, w_m2_out, w_o, norm2_w, w_up, w_down)
    fnw = final_norm_w.reshape(1, -1)
    return (_trunk(x_prompt, layers, fnw), _trunk(x_sample, layers, fnw))
```

```python
import functools

import jax
import jax.numpy as jnp
from jax import lax
from jax.experimental import pallas as pl
from jax.experimental.pallas import tpu as pltpu

F32 = jnp.float32
BF16 = jnp.bfloat16

D_MODEL = 1024
DEPTH = 2
N_DIR = 2
EPS = 1e-6
S5_WIDTH = 768
S5_GROUP = 16
S5_GROUPS = 48
S5_STATE = 64
S5_LANES = S5_GROUPS * S5_STATE
M2_INNER = 1536
M2_HEADDIM = 64
M2_HEADS = 24
M2_GROUPS = 4
M2_STATE = 128
M2_CONV = 4
M2_GN = M2_GROUPS * M2_STATE
M2_CONV_DIM = M2_INNER + 2 * M2_GN
HEADS_PER_GROUP = M2_HEADS // M2_GROUPS
D_FF = 4096

LANES = 128
SUBLANES = 8
VMEM_LIMIT_BYTES = 56 * 1024 * 1024

HALO_ROWS = 16
DT_PAD = LANES
S5_SLAB_GROUPS = LANES // S5_GROUP
S5_SLABS = S5_WIDTH // LANES
S5_SLAB_STATES = S5_SLAB_GROUPS * S5_STATE

TM_INPROJ = 256
TM_MIX = 256
TM_MLP = 512
S5_BLOCK = 16
S5_ROWS = 128
SSD_CHUNK = 128
CONV_TILE = 512
CONV_COLS = 256
CONV_SUB = 128
N_CHUNK_COLS = 512


def _cparams():
    return pltpu.CompilerParams(dimension_semantics=None, vmem_limit_bytes=VMEM_LIMIT_BYTES)


def _resident(shape):
    nd = len(shape)
    return pl.BlockSpec(shape, lambda *_: (0,) * nd, pipeline_mode=pl.Buffered(1))


def _silu(x):
    return x * (1.0 / (1.0 + jnp.exp(-x)))


def _sigmoid(x):
    return 1.0 / (1.0 + jnp.exp(-x))


def _softplus(x):
    return jnp.maximum(x, 0.0) + jnp.log1p(jnp.exp(-jnp.abs(x)))


def _rms(x, w):
    var = jnp.mean(x * x, axis=-1, keepdims=True)
    return x * lax.rsqrt(var + EPS) * w


def _dot(a, b):
    return jnp.dot(a, b, preferred_element_type=F32)


def _split3(x):
    hi = x.astype(BF16)
    r1 = x - hi.astype(F32)
    mid = r1.astype(BF16)
    lo = (r1 - mid.astype(F32)).astype(BF16)
    return hi, mid, lo


def _inproj_kernel(x_ref, nw_ref, w_ref, dtb_ref, u_ref, z_ref, xbc_ref, dt_ref, g_ref):
    h = _rms(x_ref[...], nw_ref[...]).astype(BF16)
    ident = lambda v: v
    posts = (ident, ident, ident, lambda v: _softplus(v + dtb_ref[...]), _sigmoid)
    off = 0
    for ref, post in zip((u_ref, z_ref, xbc_ref, dt_ref, g_ref), posts):
        n = ref.shape[-1]
        for c0 in range(0, n, N_CHUNK_COLS):
            c1 = min(c0 + N_CHUNK_COLS, n)
            ref[:, c0:c1] = post(_dot(h, w_ref[:, off + c0:off + c1])).astype(ref.dtype)
        off += n


def _inproj(x2d, norm_w, w_in_p, dtb_tl):
    t = x2d.shape[0]
    tm = TM_INPROJ
    widths = (S5_WIDTH, M2_INNER, M2_CONV_DIM, DT_PAD, 2 * D_MODEL)
    dtypes = (BF16, BF16, BF16, F32, BF16)
    row = lambda n: pl.BlockSpec((tm, n), lambda i: (i, 0))
    return pl.pallas_call(
        _inproj_kernel,
        grid=(t // tm,),
        in_specs=[row(D_MODEL), _resident((1, D_MODEL)), _resident(w_in_p.shape), _resident(dtb_tl.shape)],
        out_specs=[row(n) for n in widths],
        out_shape=[jax.ShapeDtypeStruct((t, n), dt) for n, dt in zip(widths, dtypes)],
        compiler_params=_cparams(),
        name="inproj",
    )(x2d, norm_w, w_in_p, dtb_tl)


def _s5_slab_halves():
    return [(k, h) for k in range(S5_SLABS) for h in range(2)]


def _s5_kernel(x_ref, p_ref, pt_ref, we_ref, t0_ref, wy_ref, are_ref, aim_ref, y_ref,
               ut_ref, ere_ref, eim_ref, yt_ref, car_ref, *, reverse):
    nb = x_ref.shape[1]
    half = S5_BLOCK // 2
    gl = S5_BLOCK * S5_GROUP

    @pl.when(pl.program_id(1) == 0)
    def _():
        car_ref[...] = jnp.zeros_like(car_ref)

    lhs = jnp.concatenate(
        [jnp.concatenate([x_ref[0, :, (half * h + r) * S5_WIDTH + k * LANES:(half * h + r) * S5_WIDTH + (k + 1) * LANES]
                          for r in range(half)], axis=1) for k, h in _s5_slab_halves()], axis=0)
    perm = _dot(lhs, p_ref[...]).astype(BF16)
    for i, (k, h) in enumerate(_s5_slab_halves()):
        for gi in range(S5_SLAB_GROUPS):
            g = k * S5_SLAB_GROUPS + gi
            ut_ref[:, g * gl + h * LANES:g * gl + (h + 1) * LANES] = perm[i * nb:(i + 1) * nb, gi * LANES:(gi + 1) * LANES]

    for gp in range(S5_GROUPS // 2):
        e = _dot(ut_ref[:, gp * 2 * gl:(gp + 1) * 2 * gl], we_ref[gp])
        ere_ref[:, gp * LANES:(gp + 1) * LANES] = e[:, :LANES]
        eim_ref[:, gp * LANES:(gp + 1) * LANES] = e[:, LANES:]

    ar, ai = are_ref[...], aim_ref[...]

    def step(i, carry):
        sr, si = carry
        row = pl.ds(nb - 1 - i if reverse else i, 1)
        er, ei = ere_ref[row, :], eim_ref[row, :]
        ere_ref[row, :] = sr
        eim_ref[row, :] = si
        return ar * sr - ai * si + er, ar * si + ai * sr + ei

    sr, si = lax.fori_loop(0, nb, step, (car_ref[0:1, :], car_ref[1:2, :]), unroll=8)
    car_ref[0:1, :] = sr
    car_ref[1:2, :] = si

    for gp in range(S5_GROUPS // 2):
        sin = jnp.concatenate([ere_ref[:, gp * LANES:(gp + 1) * LANES], eim_ref[:, gp * LANES:(gp + 1) * LANES]],
                              axis=1).astype(BF16)
        yo = _dot(sin, wy_ref[gp])
        for gi in range(2):
            g = 2 * gp + gi
            yg = _dot(ut_ref[:, g * gl:(g + 1) * gl], t0_ref[g]) + yo[:, gi * gl:(gi + 1) * gl]
            yt_ref[:, g * gl:(g + 1) * gl] = yg.astype(BF16)

    lhs2 = jnp.concatenate(
        [jnp.concatenate([yt_ref[:, (k * S5_SLAB_GROUPS + gi) * gl + h * LANES:(k * S5_SLAB_GROUPS + gi) * gl + (h + 1) * LANES]
                          for gi in range(S5_SLAB_GROUPS)], axis=1) for k, h in _s5_slab_halves()], axis=0)
    out = _dot(lhs2, pt_ref[...])
    for i, (k, h) in enumerate(_s5_slab_halves()):
        for r in range(half):
            col = (half * h + r) * S5_WIDTH + k * LANES
            y_ref[0, :, col:col + LANES] = out[i * nb:(i + 1) * nb, r * LANES:(r + 1) * LANES].astype(y_ref.dtype)


def _s5_dir(x, sp, reverse):
    b, nrows, width = x.shape
    nb = min(S5_ROWS, nrows)
    nc = nrows // nb
    blk = pl.BlockSpec((1, nb, width), (lambda i, c: (i, nc - 1 - c, 0)) if reverse else (lambda i, c: (i, c, 0)))
    weights = (sp["perm"], sp["perm_t"], sp["we"], sp["t0"], sp["wy"], sp["are"], sp["aim"])
    return pl.pallas_call(
        functools.partial(_s5_kernel, reverse=reverse),
        grid=(b, nc),
        in_specs=[blk] + [_resident(w.shape) for w in weights],
        out_specs=blk,
        out_shape=jax.ShapeDtypeStruct(x.shape, BF16),
        scratch_shapes=[pltpu.VMEM((nb, width), BF16), pltpu.VMEM((nb, S5_LANES), F32),
                        pltpu.VMEM((nb, S5_LANES), F32), pltpu.VMEM((nb, width), BF16),
                        pltpu.VMEM((SUBLANES, S5_LANES), F32)],
        compiler_params=_cparams(),
        name="s5scan",
    )(x, *weights)


def _conv_kernel(x_ref, prev_ref, next_ref, w_ref, b_ref, o_ref, buf_ref):
    j = pl.program_id(1)
    nj = pl.num_programs(1)
    tq = x_ref.shape[1]
    pad = HALO_ROWS
    buf_ref[pad:pad + tq, :] = x_ref[0]
    buf_ref[0:pad, :] = jnp.where(j == 0, jnp.zeros_like(prev_ref[0]), prev_ref[0])
    buf_ref[pad + tq:2 * pad + tq, :] = jnp.where(j == nj - 1, jnp.zeros_like(next_ref[0]), next_ref[0])
    sub = CONV_SUB
    r = lax.broadcasted_iota(jnp.int32, (sub, sub + 2 * pad), 0)
    c = lax.broadcasted_iota(jnp.int32, (sub, sub + 2 * pad), 1)
    shifts = {k: jnp.where(c == r + pad + k - 1, 1.0, 0.0).astype(BF16) for k in (0, 2, 3)}
    w = w_ref[...]
    for r0 in range(0, tq, sub):
        for c0 in range(0, x_ref.shape[2], CONV_COLS):
            cs = slice(c0, c0 + CONV_COLS)
            ext = buf_ref[r0:r0 + sub + 2 * pad, cs]
            acc = b_ref[:, cs] + w[1:2, cs] * ext[pad:pad + sub].astype(F32)
            for k in (0, 2, 3):
                acc = acc + w[k:k + 1, cs] * _dot(shifts[k], ext)
            o_ref[0, r0:r0 + sub, cs] = _silu(acc).astype(o_ref.dtype)


def _conv_silu(xbc, conv_w, conv_b):
    b, l, c = xbc.shape
    tq = min(CONV_TILE, l)
    nj = l // tq
    nh = tq // HALO_ROWS
    cur = pl.BlockSpec((1, tq, c), lambda i, j: (i, j, 0))
    prev = pl.BlockSpec((1, HALO_ROWS, c), lambda i, j: (i, jnp.maximum(j * nh - 1, 0), 0))
    nxt = pl.BlockSpec((1, HALO_ROWS, c), lambda i, j: (i, jnp.minimum((j + 1) * nh, l // HALO_ROWS - 1), 0))
    return pl.pallas_call(
        _conv_kernel,
        grid=(b, nj),
        in_specs=[cur, prev, nxt, _resident(conv_w.shape), _resident(conv_b.shape)],
        out_specs=cur,
        out_shape=jax.ShapeDtypeStruct(xbc.shape, BF16),
        scratch_shapes=[pltpu.VMEM((tq + 2 * HALO_ROWS, c), BF16)],
        compiler_params=_cparams(),
        name="conv",
    )(xbc, xbc, xbc, conv_w, conv_b)


def _ssd_chunk(xa, dt_tl, dt_hl2, alog_tl, alog_hl, st_ref, d, reverse):
    q = xa.shape[0]
    hs = slice(d * M2_HEADS, (d + 1) * M2_HEADS)
    da_tl = dt_tl * (-jnp.exp(alog_tl))
    dt_hl = dt_hl2[hs, :]
    da_hl = dt_hl * (-jnp.exp(alog_hl[hs, :]))

    r = lax.broadcasted_iota(jnp.int32, (q, q), 0)
    c = lax.broadcasted_iota(jnp.int32, (q, q), 1)
    low = jnp.where(c <= r, 1.0, 0.0).astype(BF16)
    upp = jnp.where(c >= r, 1.0, 0.0).astype(BF16)
    left, right = (upp, low) if reverse else (low, upp)
    pcs_tl = sum(_dot(left, p) for p in _split3(da_tl))
    pcs_hl = sum(_dot(p, right) for p in _split3(da_hl))
    end = 0 if reverse else q - 1
    tot_hl = pcs_hl[:, end:end + 1]
    keep = (c >= r) if reverse else (c <= r)

    lane = lax.broadcasted_iota(jnp.int32, (q, LANES), 1)
    first_head = lane < M2_HEADDIM
    lane_s = lax.broadcasted_iota(jnp.int32, (M2_STATE, LANES), 1)
    first_head_s = lane_s < M2_HEADDIM

    ys = []
    for g in range(M2_GROUPS):
        bm = xa[:, M2_INNER + g * M2_STATE:M2_INNER + (g + 1) * M2_STATE]
        cm = xa[:, M2_INNER + M2_GN + g * M2_STATE:M2_INNER + M2_GN + (g + 1) * M2_STATE]
        cb = lax.dot_general(cm, bm, (((1,), (1,)), ((), ())), preferred_element_type=F32)
        bt = bm.astype(F32).T
        st_g = st_ref[d, g]
        yoff = _dot(cm, st_g.astype(BF16))
        new_cols = []
        for jp in range(HEADS_PER_GROUP // 2):
            h0 = g * HEADS_PER_GROUP + 2 * jp
            pair = slice(h0 * M2_HEADDIM, (h0 + 2) * M2_HEADDIM)
            xp = xa[:, pair]
            xpf = xp.astype(F32)
            xbd = jnp.concatenate([jnp.where(first_head, xpf, 0.0), jnp.where(first_head, 0.0, xpf)],
                                  axis=0).astype(BF16)
            ms, ws, ecols, decs = [], [], [], []
            for h in (h0, h0 + 1):
                col = jnp.broadcast_to(pcs_tl[:, d * M2_HEADS + h:d * M2_HEADS + h + 1], (q, LANES))
                row = pcs_hl[h:h + 1, :]
                dtrow = dt_hl[h:h + 1, :]
                decay = jnp.where(keep, jnp.exp(jnp.tile(col, (1, q // LANES)) - row), 0.0)
                ms.append((cb * decay * dtrow).astype(BF16))
                ws.append((bt * (jnp.exp(tot_hl[h:h + 1, :] - row) * dtrow)).astype(BF16))
                ecols.append(jnp.exp(col))
                decs.append(jnp.exp(tot_hl[h:h + 1, :]))
            m2 = jnp.concatenate(ms, axis=1)
            w2 = jnp.concatenate(ws, axis=1)
            yslab = yoff[:, 2 * jp * M2_HEADDIM:(2 * jp + 2) * M2_HEADDIM]
            ys.append(_dot(m2, xbd) + jnp.where(first_head, ecols[0], ecols[1]) * yslab)
            sslab = st_g[:, 2 * jp * M2_HEADDIM:(2 * jp + 2) * M2_HEADDIM]
            new_cols.append(jnp.where(first_head_s, decs[0], decs[1]) * sslab + _dot(w2, xbd))
        st_ref[d, g] = jnp.concatenate(new_cols, axis=1)
    return jnp.concatenate(ys, axis=1)


def _ssd_kernel(xf_ref, xb_ref, dtf_ref, dtb_ref, dthf_ref, dthb_ref,
                alog_tl_ref, alog_hl_ref, yf_ref, yb_ref, st_ref):
    @pl.when(pl.program_id(1) == 0)
    def _():
        st_ref[...] = jnp.zeros_like(st_ref)

    consts = (alog_tl_ref[...], alog_hl_ref[...])
    yf_ref[0] = _ssd_chunk(xf_ref[0], dtf_ref[0], dthf_ref[0], *consts, st_ref, 0, False).astype(yf_ref.dtype)
    yb_ref[0] = _ssd_chunk(xb_ref[0], dtb_ref[0], dthb_ref[0], *consts, st_ref, 1, True).astype(yb_ref.dtype)


def _ssd(xact, dt_tl, dt_hl, m2p):
    b, l, _ = xact.shape
    q = min(SSD_CHUNK, l)
    nc = l // q
    fwd3 = lambda n: pl.BlockSpec((1, q, n), lambda i, c: (i, c, 0))
    bwd3 = lambda n: pl.BlockSpec((1, q, n), lambda i, c: (i, nc - 1 - c, 0))
    hl_f = pl.BlockSpec((1, N_DIR * M2_HEADS, q), lambda i, c: (i, 0, c))
    hl_b = pl.BlockSpec((1, N_DIR * M2_HEADS, q), lambda i, c: (i, 0, nc - 1 - c))
    consts = (m2p["alog_tl"], m2p["alog_hl"])
    return pl.pallas_call(
        _ssd_kernel,
        grid=(b, nc),
        in_specs=[fwd3(M2_CONV_DIM), bwd3(M2_CONV_DIM), fwd3(DT_PAD), bwd3(DT_PAD), hl_f, hl_b]
        + [_resident(w.shape) for w in consts],
        out_specs=[fwd3(M2_INNER), bwd3(M2_INNER)],
        out_shape=[jax.ShapeDtypeStruct((b, l, M2_INNER), BF16)] * 2,
        scratch_shapes=[pltpu.VMEM((N_DIR, M2_GROUPS, M2_STATE, HEADS_PER_GROUP * M2_HEADDIM), F32)],
        compiler_params=_cparams(),
        name="ssd",
    )(xact, xact, dt_tl, dt_tl, dt_hl, dt_hl, *consts)


def _mix_kernel(x_ref, g_ref, u_ref, s5f_ref, s5b_ref, xs_ref, mf_ref, mb_ref, z_ref,
                ds5_ref, wglu_ref, bglu_ref, ws5_ref, dm2_ref, nm2_ref, wm2_ref, wo_ref, o_ref):
    f = lambda ref: ref[...].astype(F32)
    y = ds5_ref[...] * f(u_ref) + f(s5f_ref) + f(s5b_ref)
    h = jax.nn.gelu(y)
    h = h * _sigmoid(_dot(h.astype(BF16), wglu_ref[...]) + bglu_ref[...])
    s5 = _dot(h.astype(BF16), ws5_ref[...])
    ym = dm2_ref[...] * f(xs_ref) + f(mf_ref) + f(mb_ref)
    gz = ym * _silu(f(z_ref))
    gw = M2_INNER // M2_GROUPS
    parts = []
    for i in range(M2_GROUPS):
        gs = gz[:, i * gw:(i + 1) * gw]
        parts.append(gs * lax.rsqrt(jnp.mean(gs * gs, axis=-1, keepdims=True) + EPS))
    gn = jnp.concatenate(parts, axis=1) * nm2_ref[...]
    m2 = _dot(gn.astype(BF16), wm2_ref[...])
    gates = f(g_ref)
    merged = gates[:, :D_MODEL] * s5 + gates[:, D_MODEL:] * m2
    o_ref[...] = x_ref[...] + _dot(merged.astype(BF16), wo_ref[...])


def _mix(x2d, gates, u, s5f, s5b, xact, mf, mb, z, mp):
    t = x2d.shape[0]
    tm = TM_MIX
    row = lambda n: pl.BlockSpec((tm, n), lambda i: (i, 0))
    weights = (mp["d_s5"], mp["w_glu"], mp["b_glu"], mp["w_s5_out"], mp["d_m2"], mp["m2_norm_w"],
               mp["w_m2_out"], mp["w_o"])
    return pl.pallas_call(
        _mix_kernel,
        grid=(t // tm,),
        in_specs=[row(D_MODEL), row(2 * D_MODEL), row(S5_WIDTH), row(S5_WIDTH), row(S5_WIDTH),
                  row(M2_INNER), row(M2_INNER), row(M2_INNER), row(M2_INNER)]
        + [_resident(w.shape) for w in weights],
        out_specs=row(D_MODEL),
        out_shape=jax.ShapeDtypeStruct((t, D_MODEL), F32),
        compiler_params=_cparams(),
        name="mix",
    )(x2d, gates, u, s5f, s5b, xact, mf, mb, z, *weights)


def _mlp_kernel(x_ref, n2_ref, wup_ref, wdn_ref, fn_ref, o_ref, *, final):
    x = x_ref[...]
    h = _rms(x, n2_ref[...]).astype(BF16)
    acc = x
    for c0 in range(0, D_FF, 2 * N_CHUNK_COLS):
        c1 = c0 + 2 * N_CHUNK_COLS
        hid = jnp.maximum(_dot(h, wup_ref[:, c0:c1]), 0.0)
        acc = acc + _dot((hid * hid).astype(BF16), wdn_ref[c0:c1, :])
    o_ref[...] = _rms(acc, fn_ref[...]) if final else acc


def _mlp(x2d, norm2_w, w_up, w_down, final_norm_w, final):
    t = x2d.shape[0]
    tm = TM_MLP
    row = pl.BlockSpec((tm, D_MODEL), lambda i: (i, 0))
    weights = (norm2_w, w_up, w_down, final_norm_w)
    return pl.pallas_call(
        functools.partial(_mlp_kernel, final=final),
        grid=(t // tm,),
        in_specs=[row] + [_resident(w.shape) for w in weights],
        out_specs=row,
        out_shape=jax.ShapeDtypeStruct((t, D_MODEL), F32),
        compiler_params=_cparams(),
        name="mlp",
    )(x2d, *weights)


def _s5_perm():
    n = S5_SLAB_GROUPS * LANES
    i = jnp.arange(n)
    r_lo, gi, c = i // LANES, (i % LANES) // S5_GROUP, i % S5_GROUP
    dst = gi * LANES + r_lo * S5_GROUP + c
    return (dst[:, None] == jnp.arange(n)[None, :]).astype(BF16)


def _pair_rows_cols(m_re, m_im, rows_are_states):
    g, a, b = m_re.shape
    z = jnp.zeros_like(m_re[0::2])
    if rows_are_states:
        top = jnp.concatenate([m_re[0::2], z], axis=2), jnp.concatenate([z, m_re[1::2]], axis=2)
        bot = jnp.concatenate([m_im[0::2], z], axis=2), jnp.concatenate([z, m_im[1::2]], axis=2)
        return jnp.concatenate([top[0], top[1], bot[0], bot[1]], axis=1)
    left = jnp.concatenate([m_re[0::2], z, m_im[0::2], z], axis=2)
    right = jnp.concatenate([z, m_re[1::2], z, m_im[1::2]], axis=2)
    return jnp.concatenate([left, right], axis=1)


def _s5_params(lam_re, lam_im, log_dt, b_re, b_im, c_re, c_im):
    rb = S5_BLOCK
    delta = jnp.exp(log_dt)[..., None]
    mag = jnp.exp(lam_re * delta)
    a_re = mag * jnp.cos(lam_im * delta)
    a_im = mag * jnp.sin(lam_im * delta)
    inv = 1.0 / (lam_re * lam_re + lam_im * lam_im)
    q_re = ((a_re - 1.0) * lam_re + a_im * lam_im) * inv
    q_im = (a_im * lam_re - (a_re - 1.0) * lam_im) * inv
    bbar_re = q_re[..., None] * b_re - q_im[..., None] * b_im
    bbar_im = q_re[..., None] * b_im + q_im[..., None] * b_re
    kk = jnp.arange(rb + 1, dtype=F32).reshape(-1, 1, 1, 1)
    pk_mag = jnp.exp(kk * (lam_re * delta))
    pk_re = pk_mag * jnp.cos(kk * (lam_im * delta))
    pk_im = pk_mag * jnp.sin(kk * (lam_im * delta))
    perm = _s5_perm()
    r = jnp.arange(rb)
    out = []
    for d in range(N_DIR):
        fwd = d == 0
        pr, pi = pk_re[:, d], pk_im[:, d]
        w_re = pr[..., None] * bbar_re[d] - pi[..., None] * bbar_im[d]
        w_im = pr[..., None] * bbar_im[d] + pi[..., None] * bbar_re[d]
        kmat = jnp.sum(c_re[d][None, :, :, :, None] * w_re[:rb, :, None, :, :]
                       - c_im[d][None, :, :, :, None] * w_im[:rb, :, None, :, :], axis=3)
        lag = (r[None, :] - r[:, None]) if fwd else (r[:, None] - r[None, :])
        t0 = jnp.where((lag >= 0)[:, :, None, None, None], kmat[jnp.clip(lag, 0, rb - 1)], 0.0)
        t0 = jnp.transpose(t0, (2, 0, 4, 1, 3)).reshape(S5_GROUPS, rb * S5_GROUP, rb * S5_GROUP)
        ke = (rb - 1 - r) if fwd else r
        we_re = jnp.transpose(w_re[ke], (1, 0, 3, 2)).reshape(S5_GROUPS, rb * S5_GROUP, S5_STATE)
        we_im = jnp.transpose(w_im[ke], (1, 0, 3, 2)).reshape(S5_GROUPS, rb * S5_GROUP, S5_STATE)
        ky = (r + 1) if fwd else (rb - r)
        g_re = c_re[d][None] * pr[ky][:, :, None, :] - c_im[d][None] * pi[ky][:, :, None, :]
        g_im = c_re[d][None] * pi[ky][:, :, None, :] + c_im[d][None] * pr[ky][:, :, None, :]
        wy_re = jnp.transpose(g_re, (1, 3, 0, 2)).reshape(S5_GROUPS, S5_STATE, rb * S5_GROUP)
        wy_im = jnp.transpose(-g_im, (1, 3, 0, 2)).reshape(S5_GROUPS, S5_STATE, rb * S5_GROUP)
        out.append({
            "perm": perm, "perm_t": perm.T,
            "we": _pair_rows_cols(we_re, we_im, False).astype(BF16),
            "t0": t0.astype(BF16),
            "wy": _pair_rows_cols(wy_re, wy_im, True).astype(BF16),
            "are": pr[rb].reshape(1, S5_LANES), "aim": pi[rb].reshape(1, S5_LANES),
        })
    return out


def _m2_params(dt_bias, a_log):
    pad = DT_PAD - N_DIR * M2_HEADS
    flat = lambda v: v.reshape(1, N_DIR * M2_HEADS)
    return {
        "dtb_tl": jnp.pad(flat(dt_bias), ((0, 0), (0, pad))),
        "alog_tl": jnp.pad(flat(a_log), ((0, 0), (0, pad))),
        "alog_hl": flat(a_log).T,
    }


def _pad_w_in(w_in):
    o_z = S5_WIDTH
    o_xbc = o_z + M2_INNER
    o_dt = o_xbc + M2_CONV_DIM
    o_g = o_dt + N_DIR * M2_HEADS
    dt = jnp.pad(w_in[:, o_dt:o_g], ((0, 0), (0, DT_PAD - N_DIR * M2_HEADS)))
    return jnp.concatenate([w_in[:, :o_dt], dt, w_in[:, o_g:]], axis=1).astype(BF16)


def _layer(x2d, b, l, lp, final_norm_w, final):
    u, z, xbc, dt, gates = _inproj(x2d, lp["norm1_w"], lp["w_in"], lp["m2"]["dtb_tl"])
    x16 = u.reshape(b, l // S5_BLOCK, S5_BLOCK * S5_WIDTH)
    s5f = _s5_dir(x16, lp["s5"][0], False)
    s5b = _s5_dir(x16, lp["s5"][1], True)
    xact = _conv_silu(xbc.reshape(b, l, M2_CONV_DIM), lp["conv_w"], lp["conv_b"])
    dt3 = dt.reshape(b, l, DT_PAD)
    dt_hl = jnp.swapaxes(dt3[:, :, :N_DIR * M2_HEADS], 1, 2)
    mf, mb = _ssd(xact, dt3, dt_hl, lp["m2"])
    t = b * l
    x1 = _mix(x2d, gates, u, s5f.reshape(t, -1), s5b.reshape(t, -1), xact.reshape(t, -1),
              mf.reshape(t, -1), mb.reshape(t, -1), z, lp)
    return _mlp(x1, lp["norm2_w"], lp["w_up"], lp["w_down"], final_norm_w, final)


def _trunk(x, layers, final_norm_w):
    b, l, _ = x.shape
    x2d = x.reshape(b * l, D_MODEL)
    for i, lp in enumerate(layers):
        x2d = _layer(x2d, b, l, lp, final_norm_w, i == len(layers) - 1)
    return x2d.reshape(b, l, D_MODEL)


def _prepare_layers(norm1_w, w_in, lam_re, lam_im, log_dt, b_re, b_im, c_re, c_im, d_s5, w_glu,
                    b_glu, w_s5_out, conv_w, conv_b, dt_bias, a_log, d_m2, m2_norm_w, w_m2_out,
                    w_o, norm2_w, w_up, w_down):
    layers = []
    for i in range(norm1_w.shape[0]):
        layers.append({
            "norm1_w": norm1_w[i].reshape(1, -1),
            "w_in": _pad_w_in(w_in[i]),
            "s5": _s5_params(lam_re[i], lam_im[i], log_dt[i], b_re[i], b_im[i], c_re[i], c_im[i]),
            "d_s5": d_s5[i].reshape(1, -1),
            "w_glu": w_glu[i].astype(BF16),
            "b_glu": b_glu[i].reshape(1, -1),
            "w_s5_out": w_s5_out[i].astype(BF16),
            "conv_w": conv_w[i],
            "conv_b": conv_b[i].reshape(1, -1),
            "m2": _m2_params(dt_bias[i], a_log[i]),
            "d_m2": jnp.repeat(d_m2[i], M2_HEADDIM).reshape(1, -1),
            "m2_norm_w": m2_norm_w[i].reshape(1, -1),
            "w_m2_out": w_m2_out[i].astype(BF16),
            "w_o": w_o[i].astype(BF16),
            "norm2_w": norm2_w[i].reshape(1, -1),
            "w_up": w_up[i].astype(BF16),
            "w_down": w_down[i].astype(BF16),
        })
    return layers


def kernel(x_prompt, x_sample, norm1_w, w_in, lam_re, lam_im, log_dt, b_re, b_im, c_re, c_im, d_s5, w_glu, b_glu, w_s5_out, conv_w, conv_b, dt_bias, a_log, d_m2, m2_norm_w, w_m2_out, w_o, norm2_w, w_up, w_down, final_norm_w):
    layers = _prepare_layers(norm1_w, w_in, lam_re, lam_im, log_dt, b_re, b_im, c_re, c_im, d_s5,
                             w_glu, b_glu, w_s5_out, conv_w, conv_b, dt_bias, a_log, d_m2,
                             m2_norm_w, w_m2_out, w_o, norm2_w, w_up, w_down)
    fnw = final_norm_w.reshape(1, -1)
    return (_trunk(x_prompt, layers, fnw), _trunk(x_sample, layers, fnw))
```

```python
import functools

import jax
import jax.numpy as jnp
from jax import lax
from jax.experimental import pallas as pl
from jax.experimental.pallas import tpu as pltpu

F32 = jnp.float32
BF16 = jnp.bfloat16

D_MODEL = 1024
DEPTH = 2
N_DIR = 2
EPS = 1e-6
S5_WIDTH = 768
S5_GROUP = 16
S5_GROUPS = 48
S5_STATE = 64
S5_LANES = S5_GROUPS * S5_STATE
M2_INNER = 1536
M2_HEADDIM = 64
M2_HEADS = 24
M2_GROUPS = 4
M2_STATE = 128
M2_CONV = 4
M2_GN = M2_GROUPS * M2_STATE
M2_CONV_DIM = M2_INNER + 2 * M2_GN
HEADS_PER_GROUP = M2_HEADS // M2_GROUPS
D_FF = 4096

LANES = 128
SUBLANES = 8
VMEM_LIMIT_BYTES = 56 * 1024 * 1024

LOG2E = 1.4426950408889634
HALO_ROWS = 16
DT_PAD = LANES
S5_SLAB_GROUPS = LANES // S5_GROUP
S5_SLABS = S5_WIDTH // LANES
S5_SLAB_STATES = S5_SLAB_GROUPS * S5_STATE

TM_INPROJ = 256
TM_MIX = 256
TM_MLP = 512
S5_BLOCK = 16
S5_ROWS = 128
SSD_CHUNK = 128
CONV_TILE = 512
CONV_COLS = 256
CONV_SUB = 128
N_CHUNK_COLS = 512


def _cparams(flags=None):
    return pltpu.CompilerParams(dimension_semantics=None, vmem_limit_bytes=VMEM_LIMIT_BYTES, flags=flags)


def _resident(shape):
    nd = len(shape)
    return pl.BlockSpec(shape, lambda *_: (0,) * nd, pipeline_mode=pl.Buffered(1))


def _silu(x):
    return x * (1.0 / (1.0 + jnp.exp(-x)))


def _sigmoid(x):
    return 1.0 / (1.0 + jnp.exp(-x))


def _softplus(x):
    return jnp.maximum(x, 0.0) + jnp.log1p(jnp.exp(-jnp.abs(x)))


def _rms(x, w):
    var = jnp.mean(x * x, axis=-1, keepdims=True)
    return x * lax.rsqrt(var + EPS) * w


def _dot(a, b):
    return jnp.dot(a, b, preferred_element_type=F32)


def _split3(x):
    hi = x.astype(BF16)
    r1 = x - hi.astype(F32)
    mid = r1.astype(BF16)
    lo = (r1 - mid.astype(F32)).astype(BF16)
    return hi, mid, lo


def _inproj_kernel(x_ref, nw_ref, w_ref, dtb_ref, alog_ref, u_ref, z_ref, xbc_ref, dt_ref, g_ref, pcs_ref):
    h = _rms(x_ref[...], nw_ref[...]).astype(BF16)
    ident = lambda v: v
    posts = (ident, ident, ident, lambda v: _softplus(v + dtb_ref[...]), _sigmoid)
    off = 0
    for ref, post in zip((u_ref, z_ref, xbc_ref, dt_ref, g_ref), posts):
        n = ref.shape[-1]
        for c0 in range(0, n, N_CHUNK_COLS):
            c1 = min(c0 + N_CHUNK_COLS, n)
            ref[:, c0:c1] = post(_dot(h, w_ref[:, off + c0:off + c1])).astype(ref.dtype)
        off += n
    q = SSD_CHUNK
    r = lax.broadcasted_iota(jnp.int32, (q, q), 0)
    c = lax.broadcasted_iota(jnp.int32, (q, q), 1)
    low = jnp.where(c <= r, 1.0, 0.0).astype(BF16)
    upp = jnp.where(c >= r, 1.0, 0.0).astype(BF16)
    fwd_col = lax.broadcasted_iota(jnp.int32, (q, DT_PAD), 1) < M2_HEADS
    for r0 in range(0, dt_ref.shape[0], q):
        parts = _split3(dt_ref[r0:r0 + q, :] * (-jnp.exp(alog_ref[...])))
        cs_f = sum(_dot(low, p) for p in parts)
        cs_b = sum(_dot(upp, p) for p in parts)
        pcs_ref[r0:r0 + q, :] = jnp.where(fwd_col, cs_f, cs_b) * LOG2E


def _inproj(x2d, norm_w, w_in_p, dtb_tl, alog_tl):
    t = x2d.shape[0]
    tm = TM_INPROJ
    assert tm % SSD_CHUNK == 0
    widths = (S5_WIDTH, M2_INNER, M2_CONV_DIM, DT_PAD, 2 * D_MODEL, DT_PAD)
    dtypes = (BF16, BF16, BF16, F32, BF16, F32)
    row = lambda n: pl.BlockSpec((tm, n), lambda i: (i, 0))
    return pl.pallas_call(
        _inproj_kernel,
        grid=(t // tm,),
        in_specs=[row(D_MODEL), _resident((1, D_MODEL)), _resident(w_in_p.shape), _resident(dtb_tl.shape),
                  _resident(alog_tl.shape)],
        out_specs=[row(n) for n in widths],
        out_shape=[jax.ShapeDtypeStruct((t, n), dt) for n, dt in zip(widths, dtypes)],
        compiler_params=_cparams(),
        name="inproj",
    )(x2d, norm_w, w_in_p, dtb_tl, alog_tl)


def _s5_slab_halves():
    return [(k, h) for k in range(S5_SLABS) for h in range(2)]


def _s5_kernel(u_ref, p_ref, pt_ref, we_ref, t0_ref, wy_ref, are_ref, aim_ref, y_ref,
               stage_ref, ut_ref, ere_ref, eim_ref, yt_ref, car_ref, *, reverse):
    nb = u_ref.shape[1] // S5_BLOCK
    half = S5_BLOCK // 2
    gl = S5_BLOCK * S5_GROUP

    @pl.when(pl.program_id(1) == 0)
    def _():
        car_ref[...] = jnp.zeros_like(car_ref)

    for k in range(S5_SLABS):
        stage_ref[k] = u_ref[0, :, k * LANES:(k + 1) * LANES].astype(F32)

    def offset_rows(k, r):
        return stage_ref[k, pl.ds(r, nb, stride=S5_BLOCK), :].astype(BF16)

    lhs = jnp.concatenate(
        [jnp.concatenate([offset_rows(k, half * h + r) for r in range(half)], axis=1)
         for k, h in _s5_slab_halves()], axis=0)
    perm = _dot(lhs, p_ref[...]).astype(BF16)
    for i, (k, h) in enumerate(_s5_slab_halves()):
        for gi in range(S5_SLAB_GROUPS):
            g = k * S5_SLAB_GROUPS + gi
            ut_ref[:, g * gl + h * LANES:g * gl + (h + 1) * LANES] = perm[i * nb:(i + 1) * nb, gi * LANES:(gi + 1) * LANES]

    for gp in range(S5_GROUPS // 2):
        e = _dot(ut_ref[:, gp * 2 * gl:(gp + 1) * 2 * gl], we_ref[gp])
        ere_ref[:, gp * LANES:(gp + 1) * LANES] = e[:, :LANES]
        eim_ref[:, gp * LANES:(gp + 1) * LANES] = e[:, LANES:]

    ar, ai = are_ref[...], aim_ref[...]

    def step(i, carry):
        sr, si = carry
        row = pl.ds(nb - 1 - i if reverse else i, 1)
        er, ei = ere_ref[row, :], eim_ref[row, :]
        ere_ref[row, :] = sr
        eim_ref[row, :] = si
        return ar * sr - ai * si + er, ar * si + ai * sr + ei

    sr, si = lax.fori_loop(0, nb, step, (car_ref[0:1, :], car_ref[1:2, :]), unroll=8)
    car_ref[0:1, :] = sr
    car_ref[1:2, :] = si

    for gp in range(S5_GROUPS // 2):
        sin = jnp.concatenate([ere_ref[:, gp * LANES:(gp + 1) * LANES], eim_ref[:, gp * LANES:(gp + 1) * LANES]],
                              axis=1).astype(BF16)
        yo = _dot(sin, wy_ref[gp])
        for gi in range(2):
            g = 2 * gp + gi
            yg = _dot(ut_ref[:, g * gl:(g + 1) * gl], t0_ref[g]) + yo[:, gi * gl:(gi + 1) * gl]
            yt_ref[:, g * gl:(g + 1) * gl] = yg.astype(BF16)

    lhs2 = jnp.concatenate(
        [jnp.concatenate([yt_ref[:, (k * S5_SLAB_GROUPS + gi) * gl + h * LANES:(k * S5_SLAB_GROUPS + gi) * gl + (h + 1) * LANES]
                          for gi in range(S5_SLAB_GROUPS)], axis=1) for k, h in _s5_slab_halves()], axis=0)
    out = _dot(lhs2, pt_ref[...])
    for i, (k, h) in enumerate(_s5_slab_halves()):
        for r in range(half):
            stage_ref[k, pl.ds(half * h + r, nb, stride=S5_BLOCK), :] = out[i * nb:(i + 1) * nb, r * LANES:(r + 1) * LANES]
    for k in range(S5_SLABS):
        y_ref[0, :, k * LANES:(k + 1) * LANES] = stage_ref[k].astype(y_ref.dtype)


def _s5_dir(u, sp, reverse):
    b, l, _ = u.shape
    nb = min(S5_ROWS, l // S5_BLOCK)
    nt = nb * S5_BLOCK
    nc = l // nt
    width = S5_BLOCK * S5_WIDTH
    blk = pl.BlockSpec((1, nt, S5_WIDTH), (lambda i, c: (i, nc - 1 - c, 0)) if reverse else (lambda i, c: (i, c, 0)))
    weights = (sp["perm"], sp["perm_t"], sp["we"], sp["t0"], sp["wy"], sp["are"], sp["aim"])
    return pl.pallas_call(
        functools.partial(_s5_kernel, reverse=reverse),
        grid=(b, nc),
        in_specs=[blk] + [_resident(w.shape) for w in weights],
        out_specs=blk,
        out_shape=jax.ShapeDtypeStruct(u.shape, BF16),
        scratch_shapes=[pltpu.VMEM((S5_SLABS, nt, LANES), F32),
                        pltpu.VMEM((nb, width), BF16), pltpu.VMEM((nb, S5_LANES), F32),
                        pltpu.VMEM((nb, S5_LANES), F32), pltpu.VMEM((nb, width), BF16),
                        pltpu.VMEM((SUBLANES, S5_LANES), F32)],
        compiler_params=_cparams(),
        name="s5scan",
    )(u, *weights)


def _conv_kernel(x_ref, prev_ref, next_ref, w_ref, b_ref, o_ref, buf_ref):
    j = pl.program_id(1)
    nj = pl.num_programs(1)
    tq = x_ref.shape[1]
    pad = HALO_ROWS
    buf_ref[pad:pad + tq, :] = x_ref[0]
    buf_ref[0:pad, :] = jnp.where(j == 0, jnp.zeros_like(prev_ref[0]), prev_ref[0])
    buf_ref[pad + tq:2 * pad + tq, :] = jnp.where(j == nj - 1, jnp.zeros_like(next_ref[0]), next_ref[0])
    sub = CONV_SUB
    r = lax.broadcasted_iota(jnp.int32, (sub, sub + 2 * pad), 0)
    c = lax.broadcasted_iota(jnp.int32, (sub, sub + 2 * pad), 1)
    shifts = {k: jnp.where(c == r + pad + k - 1, 1.0, 0.0).astype(BF16) for k in (0, 2, 3)}
    w = w_ref[...]
    for r0 in range(0, tq, sub):
        for c0 in range(0, x_ref.shape[2], CONV_COLS):
            cs = slice(c0, c0 + CONV_COLS)
            ext = buf_ref[r0:r0 + sub + 2 * pad, cs]
            acc = b_ref[:, cs] + w[1:2, cs] * ext[pad:pad + sub].astype(F32)
            for k in (0, 2, 3):
                acc = acc + w[k:k + 1, cs] * _dot(shifts[k], ext)
            o_ref[0, r0:r0 + sub, cs] = _silu(acc).astype(o_ref.dtype)


def _conv_silu(xbc, conv_w, conv_b):
    b, l, c = xbc.shape
    tq = min(CONV_TILE, l)
    nj = l // tq
    nh = tq // HALO_ROWS
    cur = pl.BlockSpec((1, tq, c), lambda i, j: (i, j, 0))
    prev = pl.BlockSpec((1, HALO_ROWS, c), lambda i, j: (i, jnp.maximum(j * nh - 1, 0), 0))
    nxt = pl.BlockSpec((1, HALO_ROWS, c), lambda i, j: (i, jnp.minimum((j + 1) * nh, l // HALO_ROWS - 1), 0))
    return pl.pallas_call(
        _conv_kernel,
        grid=(b, nj),
        in_specs=[cur, prev, nxt, _resident(conv_w.shape), _resident(conv_b.shape)],
        out_specs=cur,
        out_shape=jax.ShapeDtypeStruct(xbc.shape, BF16),
        scratch_shapes=[pltpu.VMEM((tq + 2 * HALO_ROWS, c), BF16)],
        compiler_params=_cparams(),
        name="conv",
    )(xbc, xbc, xbc, conv_w, conv_b)


def _ssd_chunk(xa, pcs_tl, pcs_hl2, dt_hl2, st_ref, d, reverse):
    q = xa.shape[0]
    hs = slice(d * M2_HEADS, (d + 1) * M2_HEADS)
    dt_hl = dt_hl2[hs, :]
    pcs_hl = pcs_hl2[hs, :]
    r = lax.broadcasted_iota(jnp.int32, (q, q), 0)
    c = lax.broadcasted_iota(jnp.int32, (q, q), 1)
    end = 0 if reverse else q - 1
    tot_hl = pcs_hl[:, end:end + 1]
    keep = (c >= r) if reverse else (c <= r)
    rowp_hl = pcs_hl - jnp.log2(dt_hl)
    wrow_hl = jnp.exp2(tot_hl - rowp_hl)
    dec_hl = jnp.exp2(tot_hl)

    lane = lax.broadcasted_iota(jnp.int32, (q, LANES), 1)
    first_head = lane < M2_HEADDIM
    first_head_row = lax.broadcasted_iota(jnp.int32, (1, LANES), 1) < M2_HEADDIM

    ys = []
    for g in range(M2_GROUPS):
        bm = xa[:, M2_INNER + g * M2_STATE:M2_INNER + (g + 1) * M2_STATE]
        cm = xa[:, M2_INNER + M2_GN + g * M2_STATE:M2_INNER + M2_GN + (g + 1) * M2_STATE]
        cb = lax.dot_general(cm, bm, (((1,), (1,)), ((), ())), preferred_element_type=F32)
        bt = bm.astype(F32).T
        st_g = st_ref[d, g]
        yoff = _dot(cm, st_g.astype(BF16))
        new_cols = []
        for jp in range(HEADS_PER_GROUP // 2):
            h0 = g * HEADS_PER_GROUP + 2 * jp
            pair = slice(h0 * M2_HEADDIM, (h0 + 2) * M2_HEADDIM)
            xp = xa[:, pair]
            xpf = xp.astype(F32)
            xbd = jnp.concatenate([jnp.where(first_head, xpf, 0.0), jnp.where(first_head, 0.0, xpf)],
                                  axis=0).astype(BF16)
            ms, ws, cols = [], [], []
            for h in (h0, h0 + 1):
                col = jnp.broadcast_to(pcs_tl[:, d * M2_HEADS + h:d * M2_HEADS + h + 1], (q, LANES))
                decay_dt = jnp.where(keep, jnp.exp2(jnp.tile(col, (1, q // LANES)) - rowp_hl[h:h + 1, :]), 0.0)
                ms.append((cb * decay_dt).astype(BF16))
                ws.append((bt * wrow_hl[h:h + 1, :]).astype(BF16))
                cols.append(col)
            m2 = jnp.concatenate(ms, axis=1)
            w2 = jnp.concatenate(ws, axis=1)
            yslab = yoff[:, 2 * jp * M2_HEADDIM:(2 * jp + 2) * M2_HEADDIM]
            ys.append(_dot(m2, xbd) + jnp.exp2(jnp.where(first_head, cols[0], cols[1])) * yslab)
            sslab = st_g[:, 2 * jp * M2_HEADDIM:(2 * jp + 2) * M2_HEADDIM]
            dec_row = jnp.where(first_head_row, dec_hl[h0:h0 + 1, :], dec_hl[h0 + 1:h0 + 2, :])
            new_cols.append(dec_row * sslab + _dot(w2, xbd))
        st_ref[d, g] = jnp.concatenate(new_cols, axis=1)
    return jnp.concatenate(ys, axis=1)


def _ssd_kernel(xf_ref, xb_ref, pf_ref, pb_ref, phf_ref, phb_ref, dthf_ref, dthb_ref, yf_ref, yb_ref, st_ref):
    @pl.when(pl.program_id(1) == 0)
    def _():
        st_ref[...] = jnp.zeros_like(st_ref)

    yf_ref[0] = _ssd_chunk(xf_ref[0], pf_ref[0], phf_ref[0], dthf_ref[0], st_ref, 0, False).astype(yf_ref.dtype)
    yb_ref[0] = _ssd_chunk(xb_ref[0], pb_ref[0], phb_ref[0], dthb_ref[0], st_ref, 1, True).astype(yb_ref.dtype)


def _ssd(xact, pcs_tl, pcs_hl, dt_hl):
    b, l, _ = xact.shape
    q = SSD_CHUNK
    nc = l // q
    fwd3 = lambda n: pl.BlockSpec((1, q, n), lambda i, c: (i, c, 0))
    bwd3 = lambda n: pl.BlockSpec((1, q, n), lambda i, c: (i, nc - 1 - c, 0))
    hl_f = pl.BlockSpec((1, N_DIR * M2_HEADS, q), lambda i, c: (i, 0, c))
    hl_b = pl.BlockSpec((1, N_DIR * M2_HEADS, q), lambda i, c: (i, 0, nc - 1 - c))
    return pl.pallas_call(
        _ssd_kernel,
        grid=(b, nc),
        in_specs=[fwd3(M2_CONV_DIM), bwd3(M2_CONV_DIM), fwd3(DT_PAD), bwd3(DT_PAD), hl_f, hl_b, hl_f, hl_b],
        out_specs=[fwd3(M2_INNER), bwd3(M2_INNER)],
        out_shape=[jax.ShapeDtypeStruct((b, l, M2_INNER), BF16)] * 2,
        scratch_shapes=[pltpu.VMEM((N_DIR, M2_GROUPS, M2_STATE, HEADS_PER_GROUP * M2_HEADDIM), F32)],
        compiler_params=_cparams(),
        name="ssd",
    )(xact, xact, pcs_tl, pcs_tl, pcs_hl, pcs_hl, dt_hl, dt_hl)


def _mix_kernel(x_ref, g_ref, u_ref, s5f_ref, s5b_ref, xs_ref, mf_ref, mb_ref, z_ref,
                ds5_ref, wglu_ref, bglu_ref, ws5_ref, dm2_ref, nm2_ref, wm2_ref, wo_ref, o_ref):
    f = lambda ref: ref[...].astype(F32)
    y = ds5_ref[...] * f(u_ref) + f(s5f_ref) + f(s5b_ref)
    h = jax.nn.gelu(y)
    h = h * _sigmoid(_dot(h.astype(BF16), wglu_ref[...]) + bglu_ref[...])
    s5 = _dot(h.astype(BF16), ws5_ref[...])
    ym = dm2_ref[...] * f(xs_ref) + f(mf_ref) + f(mb_ref)
    gz = ym * _silu(f(z_ref))
    gw = M2_INNER // M2_GROUPS
    parts = []
    for i in range(M2_GROUPS):
        gs = gz[:, i * gw:(i + 1) * gw]
        parts.append(gs * lax.rsqrt(jnp.mean(gs * gs, axis=-1, keepdims=True) + EPS))
    gn = jnp.concatenate(parts, axis=1) * nm2_ref[...]
    m2 = _dot(gn.astype(BF16), wm2_ref[...])
    gates = f(g_ref)
    merged = gates[:, :D_MODEL] * s5 + gates[:, D_MODEL:] * m2
    o_ref[...] = x_ref[...] + _dot(merged.astype(BF16), wo_ref[...])


def _mix(x2d, gates, u, s5f, s5b, xact, mf, mb, z, mp):
    t = x2d.shape[0]
    tm = TM_MIX
    row = lambda n: pl.BlockSpec((tm, n), lambda i: (i, 0))
    weights = (mp["d_s5"], mp["w_glu"], mp["b_glu"], mp["w_s5_out"], mp["d_m2"], mp["m2_norm_w"],
               mp["w_m2_out"], mp["w_o"])
    return pl.pallas_call(
        _mix_kernel,
        grid=(t // tm,),
        in_specs=[row(D_MODEL), row(2 * D_MODEL), row(S5_WIDTH), row(S5_WIDTH), row(S5_WIDTH),
                  row(M2_INNER), row(M2_INNER), row(M2_INNER), row(M2_INNER)]
        + [_resident(w.shape) for w in weights],
        out_specs=row(D_MODEL),
        out_shape=jax.ShapeDtypeStruct((t, D_MODEL), F32),
        compiler_params=_cparams(),
        name="mix",
    )(x2d, gates, u, s5f, s5b, xact, mf, mb, z, *weights)


def _mlp_kernel(x_ref, n2_ref, wup_ref, wdn_ref, fn_ref, o_ref, *, final):
    x = x_ref[...]
    h = _rms(x, n2_ref[...]).astype(BF16)
    acc = x
    for c0 in range(0, D_FF, 2 * N_CHUNK_COLS):
        c1 = c0 + 2 * N_CHUNK_COLS
        hid = jnp.maximum(_dot(h, wup_ref[:, c0:c1]), 0.0)
        acc = acc + _dot((hid * hid).astype(BF16), wdn_ref[c0:c1, :])
    o_ref[...] = _rms(acc, fn_ref[...]) if final else acc


def _mlp(x2d, norm2_w, w_up, w_down, final_norm_w, final):
    t = x2d.shape[0]
    tm = TM_MLP
    row = pl.BlockSpec((tm, D_MODEL), lambda i: (i, 0))
    weights = (norm2_w, w_up, w_down, final_norm_w)
    return pl.pallas_call(
        functools.partial(_mlp_kernel, final=final),
        grid=(t // tm,),
        in_specs=[row] + [_resident(w.shape) for w in weights],
        out_specs=row,
        out_shape=jax.ShapeDtypeStruct((t, D_MODEL), F32),
        compiler_params=_cparams(),
        name="mlp",
    )(x2d, *weights)


def _s5_perm():
    n = S5_SLAB_GROUPS * LANES
    i = jnp.arange(n)
    r_lo, gi, c = i // LANES, (i % LANES) // S5_GROUP, i % S5_GROUP
    dst = gi * LANES + r_lo * S5_GROUP + c
    return (dst[:, None] == jnp.arange(n)[None, :]).astype(BF16)


def _pair_rows_cols(m_re, m_im, rows_are_states):
    g, a, b = m_re.shape
    z = jnp.zeros_like(m_re[0::2])
    if rows_are_states:
        top = jnp.concatenate([m_re[0::2], z], axis=2), jnp.concatenate([z, m_re[1::2]], axis=2)
        bot = jnp.concatenate([m_im[0::2], z], axis=2), jnp.concatenate([z, m_im[1::2]], axis=2)
        return jnp.concatenate([top[0], top[1], bot[0], bot[1]], axis=1)
    left = jnp.concatenate([m_re[0::2], z, m_im[0::2], z], axis=2)
    right = jnp.concatenate([z, m_re[1::2], z, m_im[1::2]], axis=2)
    return jnp.concatenate([left, right], axis=1)


def _s5_params(lam_re, lam_im, log_dt, b_re, b_im, c_re, c_im):
    rb = S5_BLOCK
    delta = jnp.exp(log_dt)[..., None]
    mag = jnp.exp(lam_re * delta)
    a_re = mag * jnp.cos(lam_im * delta)
    a_im = mag * jnp.sin(lam_im * delta)
    inv = 1.0 / (lam_re * lam_re + lam_im * lam_im)
    q_re = ((a_re - 1.0) * lam_re + a_im * lam_im) * inv
    q_im = (a_im * lam_re - (a_re - 1.0) * lam_im) * inv
    bbar_re = q_re[..., None] * b_re - q_im[..., None] * b_im
    bbar_im = q_re[..., None] * b_im + q_im[..., None] * b_re
    kk = jnp.arange(rb + 1, dtype=F32).reshape(-1, 1, 1, 1)
    pk_mag = jnp.exp(kk * (lam_re * delta))
    pk_re = pk_mag * jnp.cos(kk * (lam_im * delta))
    pk_im = pk_mag * jnp.sin(kk * (lam_im * delta))
    perm = _s5_perm()
    r = jnp.arange(rb)
    out = []
    for d in range(N_DIR):
        fwd = d == 0
        pr, pi = pk_re[:, d], pk_im[:, d]
        w_re = pr[..., None] * bbar_re[d] - pi[..., None] * bbar_im[d]
        w_im = pr[..., None] * bbar_im[d] + pi[..., None] * bbar_re[d]
        kmat = jnp.sum(c_re[d][None, :, :, :, None] * w_re[:rb, :, None, :, :]
                       - c_im[d][None, :, :, :, None] * w_im[:rb, :, None, :, :], axis=3)
        lag = (r[None, :] - r[:, None]) if fwd else (r[:, None] - r[None, :])
        t0 = jnp.where((lag >= 0)[:, :, None, None, None], kmat[jnp.clip(lag, 0, rb - 1)], 0.0)
        t0 = jnp.transpose(t0, (2, 0, 4, 1, 3)).reshape(S5_GROUPS, rb * S5_GROUP, rb * S5_GROUP)
        ke = (rb - 1 - r) if fwd else r
        we_re = jnp.transpose(w_re[ke], (1, 0, 3, 2)).reshape(S5_GROUPS, rb * S5_GROUP, S5_STATE)
        we_im = jnp.transpose(w_im[ke], (1, 0, 3, 2)).reshape(S5_GROUPS, rb * S5_GROUP, S5_STATE)
        ky = (r + 1) if fwd else (rb - r)
        g_re = c_re[d][None] * pr[ky][:, :, None, :] - c_im[d][None] * pi[ky][:, :, None, :]
        g_im = c_re[d][None] * pi[ky][:, :, None, :] + c_im[d][None] * pr[ky][:, :, None, :]
        wy_re = jnp.transpose(g_re, (1, 3, 0, 2)).reshape(S5_GROUPS, S5_STATE, rb * S5_GROUP)
        wy_im = jnp.transpose(-g_im, (1, 3, 0, 2)).reshape(S5_GROUPS, S5_STATE, rb * S5_GROUP)
        out.append({
            "perm": perm, "perm_t": perm.T,
            "we": _pair_rows_cols(we_re, we_im, False).astype(BF16),
            "t0": t0.astype(BF16),
            "wy": _pair_rows_cols(wy_re, wy_im, True).astype(BF16),
            "are": pr[rb].reshape(1, S5_LANES), "aim": pi[rb].reshape(1, S5_LANES),
        })
    return out


def _m2_params(dt_bias, a_log):
    pad = DT_PAD - N_DIR * M2_HEADS
    flat = lambda v: v.reshape(1, N_DIR * M2_HEADS)
    return {
        "dtb_tl": jnp.pad(flat(dt_bias), ((0, 0), (0, pad))),
        "alog_tl": jnp.pad(flat(a_log), ((0, 0), (0, pad))),
    }


def _pad_w_in(w_in):
    o_z = S5_WIDTH
    o_xbc = o_z + M2_INNER
    o_dt = o_xbc + M2_CONV_DIM
    o_g = o_dt + N_DIR * M2_HEADS
    dt = jnp.pad(w_in[:, o_dt:o_g], ((0, 0), (0, DT_PAD - N_DIR * M2_HEADS)))
    return jnp.concatenate([w_in[:, :o_dt], dt, w_in[:, o_g:]], axis=1).astype(BF16)


def _layer(x2d, b, l, lp, final_norm_w, final):
    u, z, xbc, dt, gates, pcs = _inproj(x2d, lp["norm1_w"], lp["w_in"], lp["m2"]["dtb_tl"], lp["m2"]["alog_tl"])
    u3 = u.reshape(b, l, S5_WIDTH)
    s5f = _s5_dir(u3, lp["s5"][0], False)
    s5b = _s5_dir(u3, lp["s5"][1], True)
    xact = _conv_silu(xbc.reshape(b, l, M2_CONV_DIM), lp["conv_w"], lp["conv_b"])
    heads_on_sublanes = lambda v: jnp.swapaxes(v.reshape(b, l, DT_PAD)[:, :, :N_DIR * M2_HEADS], 1, 2)
    mf, mb = _ssd(xact, pcs.reshape(b, l, DT_PAD), heads_on_sublanes(pcs), heads_on_sublanes(dt))
    t = b * l
    x1 = _mix(x2d, gates, u, s5f.reshape(t, -1), s5b.reshape(t, -1), xact.reshape(t, -1),
              mf.reshape(t, -1), mb.reshape(t, -1), z, lp)
    return _mlp(x1, lp["norm2_w"], lp["w_up"], lp["w_down"], final_norm_w, final)


def _trunk(x, layers, final_norm_w):
    b, l, _ = x.shape
    x2d = x.reshape(b * l, D_MODEL)
    for i, lp in enumerate(layers):
        x2d = _layer(x2d, b, l, lp, final_norm_w, i == len(layers) - 1)
    return x2d.reshape(b, l, D_MODEL)


def _prepare_layers(norm1_w, w_in, lam_re, lam_im, log_dt, b_re, b_im, c_re, c_im, d_s5, w_glu,
                    b_glu, w_s5_out, conv_w, conv_b, dt_bias, a_log, d_m2, m2_norm_w, w_m2_out,
                    w_o, norm2_w, w_up, w_down):
    layers = []
    for i in range(norm1_w.shape[0]):
        layers.append({
            "norm1_w": norm1_w[i].reshape(1, -1),
            "w_in": _pad_w_in(w_in[i]),
            "s5": _s5_params(lam_re[i], lam_im[i], log_dt[i], b_re[i], b_im[i], c_re[i], c_im[i]),
            "d_s5": d_s5[i].reshape(1, -1),
            "w_glu": w_glu[i].astype(BF16),
            "b_glu": b_glu[i].reshape(1, -1),
            "w_s5_out": w_s5_out[i].astype(BF16),
            "conv_w": conv_w[i],
            "conv_b": conv_b[i].reshape(1, -1),
            "m2": _m2_params(dt_bias[i], a_log[i]),
            "d_m2": jnp.repeat(d_m2[i], M2_HEADDIM).reshape(1, -1),
            "m2_norm_w": m2_norm_w[i].reshape(1, -1),
            "w_m2_out": w_m2_out[i].astype(BF16),
            "w_o": w_o[i].astype(BF16),
            "norm2_w": norm2_w[i].reshape(1, -1),
            "w_up": w_up[i].astype(BF16),
            "w_down": w_down[i].astype(BF16),
        })
    return layers


def kernel(x_prompt, x_sample, norm1_w, w_in, lam_re, lam_im, log_dt, b_re, b_im, c_re, c_im, d_s5, w_glu, b_glu, w_s5_out, conv_w, conv_b, dt_bias, a_log, d_m2, m2_norm_w, w_m2_out, w_o, norm2_w, w_up, w_down, final_norm_w):
    layers = _prepare_layers(norm1_w, w_in, lam_re, lam_im, log_dt, b_re, b_im, c_re, c_im, d_s5,
                             w_glu, b_glu, w_s5_out, conv_w, conv_b, dt_bias, a_log, d_m2,
                             m2_norm_w, w_m2_out, w_o, norm2_w, w_up, w_down)
    fnw = final_norm_w.reshape(1, -1)
    return (_trunk(x_prompt, layers, fnw), _trunk(x_sample, layers, fnw))
```

```python
import functools

import jax
import jax.numpy as jnp
from jax import lax
from jax.experimental import pallas as pl
from jax.experimental.pallas import tpu as pltpu

F32 = jnp.float32
BF16 = jnp.bfloat16

D_MODEL = 1024
DEPTH = 2
N_DIR = 2
EPS = 1e-6
S5_WIDTH = 768
S5_GROUP = 16
S5_GROUPS = 48
S5_STATE = 64
S5_LANES = S5_GROUPS * S5_STATE
M2_INNER = 1536
M2_HEADDIM = 64
M2_HEADS = 24
M2_GROUPS = 4
M2_STATE = 128
M2_CONV = 4
M2_GN = M2_GROUPS * M2_STATE
M2_CONV_DIM = M2_INNER + 2 * M2_GN
HEADS_PER_GROUP = M2_HEADS // M2_GROUPS
D_FF = 4096

LANES = 128
SUBLANES = 8
VMEM_LIMIT_BYTES = 56 * 1024 * 1024

LOG2E = 1.4426950408889634
DT_PAD = LANES
S5_SLAB_GROUPS = LANES // S5_GROUP
S5_SLABS = S5_WIDTH // LANES
S5_SLAB_STATES = S5_SLAB_GROUPS * S5_STATE

TM_INPROJ = 256
TM_MIX = 256
TM_MLP = 512
S5_BLOCK = 16
S5_ROWS = 128
SSD_CHUNK = 128
CONV_COLS = 256
CONV_SUB = 128
N_CHUNK_COLS = 512


def _cparams(flags=None):
    return pltpu.CompilerParams(dimension_semantics=None, vmem_limit_bytes=VMEM_LIMIT_BYTES, flags=flags)


def _resident(shape):
    nd = len(shape)
    return pl.BlockSpec(shape, lambda *_: (0,) * nd, pipeline_mode=pl.Buffered(1))


def _silu(x):
    return x * (1.0 / (1.0 + jnp.exp(-x)))


def _sigmoid(x):
    return 1.0 / (1.0 + jnp.exp(-x))


def _softplus(x):
    return jnp.maximum(x, 0.0) + jnp.log1p(jnp.exp(-jnp.abs(x)))


def _rms(x, w):
    var = jnp.mean(x * x, axis=-1, keepdims=True)
    return x * lax.rsqrt(var + EPS) * w


def _dot(a, b):
    return jnp.dot(a, b, preferred_element_type=F32)


def _split3(x):
    hi = x.astype(BF16)
    r1 = x - hi.astype(F32)
    mid = r1.astype(BF16)
    lo = (r1 - mid.astype(F32)).astype(BF16)
    return hi, mid, lo


def _conv_silu_rows(xe, tm, first, last, w, b):
    before = jnp.where(first, 0.0, xe[tm:tm + SUBLANES])
    after = jnp.where(last, 0.0, xe[tm + SUBLANES:tm + 2 * SUBLANES])
    ext = jnp.concatenate([before, xe[:tm], after], axis=0)
    outs = []
    for r0 in range(0, tm, CONV_SUB):
        blk = ext[r0:r0 + CONV_SUB + 2 * SUBLANES]
        nr = blk.shape[0]
        mid = slice(SUBLANES, SUBLANES + CONV_SUB)
        acc = (w[0:1] * pltpu.roll(blk, 1, 0)[mid] + w[1:2] * blk[mid] + w[2:3] * pltpu.roll(blk, nr - 1, 0)[mid]
               + w[3:4] * pltpu.roll(blk, nr - 2, 0)[mid] + b)
        outs.append(_silu(acc))
    return jnp.concatenate(outs, axis=0)


def _inproj_kernel(x_ref, xp_ref, xn_ref, nw_ref, w_ref, dtb_ref, alog_ref, cw_ref, cb_ref,
                   u_ref, z_ref, xact_ref, dt_ref, g_ref, pcs_ref, *, tiles_per_seq):
    tm = x_ref.shape[0]
    i = pl.program_id(0)
    first = lax.rem(i, tiles_per_seq) == 0
    last = lax.rem(i + 1, tiles_per_seq) == 0
    he = _rms(jnp.concatenate([x_ref[...], xp_ref[...], xn_ref[...]], axis=0), nw_ref[...]).astype(BF16)
    h = he[:tm]
    ident = lambda v: v
    posts = (ident, ident, None, lambda v: _softplus(v + dtb_ref[...]), _sigmoid)
    off = 0
    for ref, post in zip((u_ref, z_ref, xact_ref, dt_ref, g_ref), posts):
        n = ref.shape[-1]
        step = CONV_COLS if post is None else N_CHUNK_COLS
        for c0 in range(0, n, step):
            c1 = min(c0 + step, n)
            wc = w_ref[:, off + c0:off + c1]
            if post is None:
                val = _conv_silu_rows(_dot(he, wc), tm, first, last, cw_ref[:, c0:c1], cb_ref[:, c0:c1])
            else:
                val = post(_dot(h, wc))
            ref[:, c0:c1] = val.astype(ref.dtype)
        off += n
    q = SSD_CHUNK
    r = lax.broadcasted_iota(jnp.int32, (q, q), 0)
    c = lax.broadcasted_iota(jnp.int32, (q, q), 1)
    low = jnp.where(c <= r, 1.0, 0.0).astype(BF16)
    upp = jnp.where(c >= r, 1.0, 0.0).astype(BF16)
    fwd_col = lax.broadcasted_iota(jnp.int32, (q, DT_PAD), 1) < M2_HEADS
    for r0 in range(0, dt_ref.shape[0], q):
        parts = _split3(dt_ref[r0:r0 + q, :] * (-jnp.exp(alog_ref[...])))
        cs_f = sum(_dot(low, p) for p in parts)
        cs_b = sum(_dot(upp, p) for p in parts)
        pcs_ref[r0:r0 + q, :] = jnp.where(fwd_col, cs_f, cs_b) * LOG2E


def _inproj(x2d, seq_len, norm_w, w_in_p, dtb_tl, alog_tl, conv_w, conv_b):
    t = x2d.shape[0]
    tm = TM_INPROJ
    assert tm % SSD_CHUNK == 0 and seq_len % tm == 0
    widths = (S5_WIDTH, M2_INNER, M2_CONV_DIM, DT_PAD, 2 * D_MODEL, DT_PAD)
    dtypes = (BF16, BF16, BF16, F32, BF16, F32)
    row = lambda n: pl.BlockSpec((tm, n), lambda i: (i, 0))
    n8 = tm // SUBLANES
    before = pl.BlockSpec((SUBLANES, D_MODEL), lambda i: (jnp.maximum(i * n8 - 1, 0), 0))
    after = pl.BlockSpec((SUBLANES, D_MODEL), lambda i: (jnp.minimum((i + 1) * n8, t // SUBLANES - 1), 0))
    consts = (norm_w, w_in_p, dtb_tl, alog_tl, conv_w, conv_b)
    return pl.pallas_call(
        functools.partial(_inproj_kernel, tiles_per_seq=seq_len // tm),
        grid=(t // tm,),
        in_specs=[row(D_MODEL), before, after] + [_resident(w.shape) for w in consts],
        out_specs=[row(n) for n in widths],
        out_shape=[jax.ShapeDtypeStruct((t, n), dt) for n, dt in zip(widths, dtypes)],
        compiler_params=_cparams(),
        name="inproj",
    )(x2d, x2d, x2d, *consts)


def _s5_slab_halves():
    return [(k, h) for k in range(S5_SLABS) for h in range(2)]


def _s5_kernel(u_ref, p_ref, pt_ref, we_ref, t0_ref, wy_ref, are_ref, aim_ref, y_ref,
               stage_ref, ut_ref, ere_ref, eim_ref, yt_ref, car_ref, *, reverse):
    nb = u_ref.shape[1] // S5_BLOCK
    half = S5_BLOCK // 2
    gl = S5_BLOCK * S5_GROUP

    @pl.when(pl.program_id(1) == 0)
    def _():
        car_ref[...] = jnp.zeros_like(car_ref)

    for k in range(S5_SLABS):
        stage_ref[k] = u_ref[0, :, k * LANES:(k + 1) * LANES].astype(F32)

    def offset_rows(k, r):
        return stage_ref[k, pl.ds(r, nb, stride=S5_BLOCK), :].astype(BF16)

    lhs = jnp.concatenate(
        [jnp.concatenate([offset_rows(k, half * h + r) for r in range(half)], axis=1)
         for k, h in _s5_slab_halves()], axis=0)
    perm = _dot(lhs, p_ref[...]).astype(BF16)
    for i, (k, h) in enumerate(_s5_slab_halves()):
        for gi in range(S5_SLAB_GROUPS):
            g = k * S5_SLAB_GROUPS + gi
            ut_ref[:, g * gl + h * LANES:g * gl + (h + 1) * LANES] = perm[i * nb:(i + 1) * nb, gi * LANES:(gi + 1) * LANES]

    for gp in range(S5_GROUPS // 2):
        e = _dot(ut_ref[:, gp * 2 * gl:(gp + 1) * 2 * gl], we_ref[gp])
        ere_ref[:, gp * LANES:(gp + 1) * LANES] = e[:, :LANES]
        eim_ref[:, gp * LANES:(gp + 1) * LANES] = e[:, LANES:]

    ar, ai = are_ref[...], aim_ref[...]

    def step(i, carry):
        sr, si = carry
        row = pl.ds(nb - 1 - i if reverse else i, 1)
        er, ei = ere_ref[row, :], eim_ref[row, :]
        ere_ref[row, :] = sr
        eim_ref[row, :] = si
        return ar * sr - ai * si + er, ar * si + ai * sr + ei

    sr, si = lax.fori_loop(0, nb, step, (car_ref[0:1, :], car_ref[1:2, :]), unroll=8)
    car_ref[0:1, :] = sr
    car_ref[1:2, :] = si

    for gp in range(S5_GROUPS // 2):
        sin = jnp.concatenate([ere_ref[:, gp * LANES:(gp + 1) * LANES], eim_ref[:, gp * LANES:(gp + 1) * LANES]],
                              axis=1).astype(BF16)
        yo = _dot(sin, wy_ref[gp])
        for gi in range(2):
            g = 2 * gp + gi
            yg = _dot(ut_ref[:, g * gl:(g + 1) * gl], t0_ref[g]) + yo[:, gi * gl:(gi + 1) * gl]
            yt_ref[:, g * gl:(g + 1) * gl] = yg.astype(BF16)

    lhs2 = jnp.concatenate(
        [jnp.concatenate([yt_ref[:, (k * S5_SLAB_GROUPS + gi) * gl + h * LANES:(k * S5_SLAB_GROUPS + gi) * gl + (h + 1) * LANES]
                          for gi in range(S5_SLAB_GROUPS)], axis=1) for k, h in _s5_slab_halves()], axis=0)
    out = _dot(lhs2, pt_ref[...])
    for i, (k, h) in enumerate(_s5_slab_halves()):
        for r in range(half):
            stage_ref[k, pl.ds(half * h + r, nb, stride=S5_BLOCK), :] = out[i * nb:(i + 1) * nb, r * LANES:(r + 1) * LANES]
    for k in range(S5_SLABS):
        y_ref[0, :, k * LANES:(k + 1) * LANES] = stage_ref[k].astype(y_ref.dtype)


def _s5_dir(u, sp, reverse):
    b, l, _ = u.shape
    nb = min(S5_ROWS, l // S5_BLOCK)
    nt = nb * S5_BLOCK
    nc = l // nt
    width = S5_BLOCK * S5_WIDTH
    blk = pl.BlockSpec((1, nt, S5_WIDTH), (lambda i, c: (i, nc - 1 - c, 0)) if reverse else (lambda i, c: (i, c, 0)))
    weights = (sp["perm"], sp["perm_t"], sp["we"], sp["t0"], sp["wy"], sp["are"], sp["aim"])
    return pl.pallas_call(
        functools.partial(_s5_kernel, reverse=reverse),
        grid=(b, nc),
        in_specs=[blk] + [_resident(w.shape) for w in weights],
        out_specs=blk,
        out_shape=jax.ShapeDtypeStruct(u.shape, BF16),
        scratch_shapes=[pltpu.VMEM((S5_SLABS, nt, LANES), F32),
                        pltpu.VMEM((nb, width), BF16), pltpu.VMEM((nb, S5_LANES), F32),
                        pltpu.VMEM((nb, S5_LANES), F32), pltpu.VMEM((nb, width), BF16),
                        pltpu.VMEM((SUBLANES, S5_LANES), F32)],
        compiler_params=_cparams(),
        name="s5scan",
    )(u, *weights)


def _ssd_chunk(xa, pcs_tl, pcs_hl2, dt_hl2, st_ref, d, reverse):
    q = xa.shape[0]
    hs = slice(d * M2_HEADS, (d + 1) * M2_HEADS)
    dt_hl = dt_hl2[hs, :]
    pcs_hl = pcs_hl2[hs, :]
    r = lax.broadcasted_iota(jnp.int32, (q, q), 0)
    c = lax.broadcasted_iota(jnp.int32, (q, q), 1)
    end = 0 if reverse else q - 1
    tot_hl = pcs_hl[:, end:end + 1]
    keep = (c >= r) if reverse else (c <= r)
    rowp_hl = pcs_hl - jnp.log2(dt_hl)
    wrow_hl = jnp.exp2(tot_hl - rowp_hl)
    dec_hl = jnp.exp2(tot_hl)

    lane = lax.broadcasted_iota(jnp.int32, (q, LANES), 1)
    first_head = lane < M2_HEADDIM
    first_head_row = lax.broadcasted_iota(jnp.int32, (1, LANES), 1) < M2_HEADDIM

    ys = []
    for g in range(M2_GROUPS):
        bm = xa[:, M2_INNER + g * M2_STATE:M2_INNER + (g + 1) * M2_STATE]
        cm = xa[:, M2_INNER + M2_GN + g * M2_STATE:M2_INNER + M2_GN + (g + 1) * M2_STATE]
        cb = lax.dot_general(cm, bm, (((1,), (1,)), ((), ())), preferred_element_type=F32)
        bt = bm.astype(F32).T
        st_g = st_ref[d, g]
        yoff = _dot(cm, st_g.astype(BF16))
        new_cols = []
        for jp in range(HEADS_PER_GROUP // 2):
            h0 = g * HEADS_PER_GROUP + 2 * jp
            pair = slice(h0 * M2_HEADDIM, (h0 + 2) * M2_HEADDIM)
            xp = xa[:, pair]
            xpf = xp.astype(F32)
            xbd = jnp.concatenate([jnp.where(first_head, xpf, 0.0), jnp.where(first_head, 0.0, xpf)],
                                  axis=0).astype(BF16)
            ms, ws, cols = [], [], []
            for h in (h0, h0 + 1):
                col = jnp.broadcast_to(pcs_tl[:, d * M2_HEADS + h:d * M2_HEADS + h + 1], (q, LANES))
                decay_dt = jnp.where(keep, jnp.exp2(jnp.tile(col, (1, q // LANES)) - rowp_hl[h:h + 1, :]), 0.0)
                ms.append((cb * decay_dt).astype(BF16))
                ws.append((bt * wrow_hl[h:h + 1, :]).astype(BF16))
                cols.append(col)
            m2 = jnp.concatenate(ms, axis=1)
            w2 = jnp.concatenate(ws, axis=1)
            yslab = yoff[:, 2 * jp * M2_HEADDIM:(2 * jp + 2) * M2_HEADDIM]
            ys.append(_dot(m2, xbd) + jnp.exp2(jnp.where(first_head, cols[0], cols[1])) * yslab)
            sslab = st_g[:, 2 * jp * M2_HEADDIM:(2 * jp + 2) * M2_HEADDIM]
            dec_row = jnp.where(first_head_row, dec_hl[h0:h0 + 1, :], dec_hl[h0 + 1:h0 + 2, :])
            new_cols.append(dec_row * sslab + _dot(w2, xbd))
        st_ref[d, g] = jnp.concatenate(new_cols, axis=1)
    return jnp.concatenate(ys, axis=1)


def _ssd_kernel(xf_ref, xb_ref, pf_ref, pb_ref, phf_ref, phb_ref, dthf_ref, dthb_ref, yf_ref, yb_ref, st_ref):
    @pl.when(pl.program_id(1) == 0)
    def _():
        st_ref[...] = jnp.zeros_like(st_ref)

    yf_ref[0] = _ssd_chunk(xf_ref[0], pf_ref[0], phf_ref[0], dthf_ref[0], st_ref, 0, False).astype(yf_ref.dtype)
    yb_ref[0] = _ssd_chunk(xb_ref[0], pb_ref[0], phb_ref[0], dthb_ref[0], st_ref, 1, True).astype(yb_ref.dtype)


def _ssd(xact, pcs_tl, pcs_hl, dt_hl):
    b, l, _ = xact.shape
    q = SSD_CHUNK
    nc = l // q
    fwd3 = lambda n: pl.BlockSpec((1, q, n), lambda i, c: (i, c, 0))
    bwd3 = lambda n: pl.BlockSpec((1, q, n), lambda i, c: (i, nc - 1 - c, 0))
    hl_f = pl.BlockSpec((1, N_DIR * M2_HEADS, q), lambda i, c: (i, 0, c))
    hl_b = pl.BlockSpec((1, N_DIR * M2_HEADS, q), lambda i, c: (i, 0, nc - 1 - c))
    return pl.pallas_call(
        _ssd_kernel,
        grid=(b, nc),
        in_specs=[fwd3(M2_CONV_DIM), bwd3(M2_CONV_DIM), fwd3(DT_PAD), bwd3(DT_PAD), hl_f, hl_b, hl_f, hl_b],
        out_specs=[fwd3(M2_INNER), bwd3(M2_INNER)],
        out_shape=[jax.ShapeDtypeStruct((b, l, M2_INNER), BF16)] * 2,
        scratch_shapes=[pltpu.VMEM((N_DIR, M2_GROUPS, M2_STATE, HEADS_PER_GROUP * M2_HEADDIM), F32)],
        compiler_params=_cparams(),
        name="ssd",
    )(xact, xact, pcs_tl, pcs_tl, pcs_hl, pcs_hl, dt_hl, dt_hl)


def _mix_kernel(x_ref, g_ref, u_ref, s5f_ref, s5b_ref, xs_ref, mf_ref, mb_ref, z_ref,
                ds5_ref, wglu_ref, bglu_ref, ws5_ref, dm2_ref, nm2_ref, wm2_ref, wo_ref, o_ref):
    f = lambda ref: ref[...].astype(F32)
    y = ds5_ref[...] * f(u_ref) + f(s5f_ref) + f(s5b_ref)
    h = jax.nn.gelu(y)
    h = h * _sigmoid(_dot(h.astype(BF16), wglu_ref[...]) + bglu_ref[...])
    s5 = _dot(h.astype(BF16), ws5_ref[...])
    ym = dm2_ref[...] * f(xs_ref) + f(mf_ref) + f(mb_ref)
    gz = ym * _silu(f(z_ref))
    gw = M2_INNER // M2_GROUPS
    parts = []
    for i in range(M2_GROUPS):
        gs = gz[:, i * gw:(i + 1) * gw]
        parts.append(gs * lax.rsqrt(jnp.mean(gs * gs, axis=-1, keepdims=True) + EPS))
    gn = jnp.concatenate(parts, axis=1) * nm2_ref[...]
    m2 = _dot(gn.astype(BF16), wm2_ref[...])
    gates = f(g_ref)
    merged = gates[:, :D_MODEL] * s5 + gates[:, D_MODEL:] * m2
    o_ref[...] = x_ref[...] + _dot(merged.astype(BF16), wo_ref[...])


def _mix(x2d, gates, u, s5f, s5b, xact, mf, mb, z, mp):
    t = x2d.shape[0]
    tm = TM_MIX
    row = lambda n: pl.BlockSpec((tm, n), lambda i: (i, 0))
    weights = (mp["d_s5"], mp["w_glu"], mp["b_glu"], mp["w_s5_out"], mp["d_m2"], mp["m2_norm_w"],
               mp["w_m2_out"], mp["w_o"])
    return pl.pallas_call(
        _mix_kernel,
        grid=(t // tm,),
        in_specs=[row(D_MODEL), row(2 * D_MODEL), row(S5_WIDTH), row(S5_WIDTH), row(S5_WIDTH),
                  row(M2_INNER), row(M2_INNER), row(M2_INNER), row(M2_INNER)]
        + [_resident(w.shape) for w in weights],
        out_specs=row(D_MODEL),
        out_shape=jax.ShapeDtypeStruct((t, D_MODEL), F32),
        compiler_params=_cparams(),
        name="mix",
    )(x2d, gates, u, s5f, s5b, xact, mf, mb, z, *weights)


def _mlp_kernel(x_ref, n2_ref, wup_ref, wdn_ref, fn_ref, o_ref, *, final):
    x = x_ref[...]
    h = _rms(x, n2_ref[...]).astype(BF16)
    acc = x
    for c0 in range(0, D_FF, 2 * N_CHUNK_COLS):
        c1 = c0 + 2 * N_CHUNK_COLS
        hid = jnp.maximum(_dot(h, wup_ref[:, c0:c1]), 0.0)
        acc = acc + _dot((hid * hid).astype(BF16), wdn_ref[c0:c1, :])
    o_ref[...] = _rms(acc, fn_ref[...]) if final else acc


def _mlp(x2d, norm2_w, w_up, w_down, final_norm_w, final):
    t = x2d.shape[0]
    tm = TM_MLP
    row = pl.BlockSpec((tm, D_MODEL), lambda i: (i, 0))
    weights = (norm2_w, w_up, w_down, final_norm_w)
    return pl.pallas_call(
        functools.partial(_mlp_kernel, final=final),
        grid=(t // tm,),
        in_specs=[row] + [_resident(w.shape) for w in weights],
        out_specs=row,
        out_shape=jax.ShapeDtypeStruct((t, D_MODEL), F32),
        compiler_params=_cparams(),
        name="mlp",
    )(x2d, *weights)


def _s5_perm():
    n = S5_SLAB_GROUPS * LANES
    i = jnp.arange(n)
    r_lo, gi, c = i // LANES, (i % LANES) // S5_GROUP, i % S5_GROUP
    dst = gi * LANES + r_lo * S5_GROUP + c
    return (dst[:, None] == jnp.arange(n)[None, :]).astype(BF16)


def _pair_rows_cols(m_re, m_im, rows_are_states):
    g, a, b = m_re.shape
    z = jnp.zeros_like(m_re[0::2])
    if rows_are_states:
        top = jnp.concatenate([m_re[0::2], z], axis=2), jnp.concatenate([z, m_re[1::2]], axis=2)
        bot = jnp.concatenate([m_im[0::2], z], axis=2), jnp.concatenate([z, m_im[1::2]], axis=2)
        return jnp.concatenate([top[0], top[1], bot[0], bot[1]], axis=1)
    left = jnp.concatenate([m_re[0::2], z, m_im[0::2], z], axis=2)
    right = jnp.concatenate([z, m_re[1::2], z, m_im[1::2]], axis=2)
    return jnp.concatenate([left, right], axis=1)


def _s5_params(lam_re, lam_im, log_dt, b_re, b_im, c_re, c_im):
    rb = S5_BLOCK
    delta = jnp.exp(log_dt)[..., None]
    mag = jnp.exp(lam_re * delta)
    a_re = mag * jnp.cos(lam_im * delta)
    a_im = mag * jnp.sin(lam_im * delta)
    inv = 1.0 / (lam_re * lam_re + lam_im * lam_im)
    q_re = ((a_re - 1.0) * lam_re + a_im * lam_im) * inv
    q_im = (a_im * lam_re - (a_re - 1.0) * lam_im) * inv
    bbar_re = q_re[..., None] * b_re - q_im[..., None] * b_im
    bbar_im = q_re[..., None] * b_im + q_im[..., None] * b_re
    kk = jnp.arange(rb + 1, dtype=F32).reshape(-1, 1, 1, 1)
    pk_mag = jnp.exp(kk * (lam_re * delta))
    pk_re = pk_mag * jnp.cos(kk * (lam_im * delta))
    pk_im = pk_mag * jnp.sin(kk * (lam_im * delta))
    perm = _s5_perm()
    r = jnp.arange(rb)
    out = []
    for d in range(N_DIR):
        fwd = d == 0
        pr, pi = pk_re[:, d], pk_im[:, d]
        w_re = pr[..., None] * bbar_re[d] - pi[..., None] * bbar_im[d]
        w_im = pr[..., None] * bbar_im[d] + pi[..., None] * bbar_re[d]
        k3 = kk[:rb, 0]
        ni_mag = jnp.exp(-k3 * (lam_re[d] * delta[d]))
        ni_re = ni_mag * jnp.cos(k3 * (lam_im[d] * delta[d]))
        ni_im = -ni_mag * jnp.sin(k3 * (lam_im[d] * delta[d]))
        src_re = ni_re[..., None] * bbar_re[d] - ni_im[..., None] * bbar_im[d]
        src_im = ni_re[..., None] * bbar_im[d] + ni_im[..., None] * bbar_re[d]
        dst_re = c_re[d][None] * pr[:rb, :, None, :] - c_im[d][None] * pi[:rb, :, None, :]
        dst_im = c_re[d][None] * pi[:rb, :, None, :] + c_im[d][None] * pr[:rb, :, None, :]
        if not fwd:
            src_re, src_im = w_re[:rb], w_im[:rb]
            dst_re = c_re[d][None] * ni_re[:, :, None, :] - c_im[d][None] * ni_im[:, :, None, :]
            dst_im = c_re[d][None] * ni_im[:, :, None, :] + c_im[d][None] * ni_re[:, :, None, :]
        src = jnp.concatenate([src_re, src_im], axis=2)
        dst = jnp.concatenate([dst_re, -dst_im], axis=3)
        t0 = jnp.einsum("rgpc,sgdp->grcsd", src, dst, precision=lax.Precision.HIGHEST)
        causal = (r[None, :] >= r[:, None]) if fwd else (r[None, :] <= r[:, None])
        t0 = jnp.where(causal[None, :, None, :, None], t0, 0.0).reshape(S5_GROUPS, rb * S5_GROUP, rb * S5_GROUP)
        ke = (rb - 1 - r) if fwd else r
        we_re = jnp.transpose(w_re[ke], (1, 0, 3, 2)).reshape(S5_GROUPS, rb * S5_GROUP, S5_STATE)
        we_im = jnp.transpose(w_im[ke], (1, 0, 3, 2)).reshape(S5_GROUPS, rb * S5_GROUP, S5_STATE)
        ky = (r + 1) if fwd else (rb - r)
        g_re = c_re[d][None] * pr[ky][:, :, None, :] - c_im[d][None] * pi[ky][:, :, None, :]
        g_im = c_re[d][None] * pi[ky][:, :, None, :] + c_im[d][None] * pr[ky][:, :, None, :]
        wy_re = jnp.transpose(g_re, (1, 3, 0, 2)).reshape(S5_GROUPS, S5_STATE, rb * S5_GROUP)
        wy_im = jnp.transpose(-g_im, (1, 3, 0, 2)).reshape(S5_GROUPS, S5_STATE, rb * S5_GROUP)
        out.append({
            "perm": perm, "perm_t": perm.T,
            "we": _pair_rows_cols(we_re, we_im, False).astype(BF16),
            "t0": t0.astype(BF16),
            "wy": _pair_rows_cols(wy_re, wy_im, True).astype(BF16),
            "are": pr[rb].reshape(1, S5_LANES), "aim": pi[rb].reshape(1, S5_LANES),
        })
    return out


def _m2_params(dt_bias, a_log):
    pad = DT_PAD - N_DIR * M2_HEADS
    flat = lambda v: v.reshape(1, N_DIR * M2_HEADS)
    return {
        "dtb_tl": jnp.pad(flat(dt_bias), ((0, 0), (0, pad))),
        "alog_tl": jnp.pad(flat(a_log), ((0, 0), (0, pad))),
    }


def _pad_w_in(w_in):
    o_z = S5_WIDTH
    o_xbc = o_z + M2_INNER
    o_dt = o_xbc + M2_CONV_DIM
    o_g = o_dt + N_DIR * M2_HEADS
    dt = jnp.pad(w_in[:, o_dt:o_g], ((0, 0), (0, DT_PAD - N_DIR * M2_HEADS)))
    return jnp.concatenate([w_in[:, :o_dt], dt, w_in[:, o_g:]], axis=1).astype(BF16)


def _layer(x2d, b, l, lp, final_norm_w, final):
    u, z, xact, dt, gates, pcs = _inproj(x2d, l, lp["norm1_w"], lp["w_in"], lp["m2"]["dtb_tl"],
                                         lp["m2"]["alog_tl"], lp["conv_w"], lp["conv_b"])
    u3 = u.reshape(b, l, S5_WIDTH)
    s5f = _s5_dir(u3, lp["s5"][0], False)
    s5b = _s5_dir(u3, lp["s5"][1], True)
    xact = xact.reshape(b, l, M2_CONV_DIM)
    heads_on_sublanes = lambda v: jnp.swapaxes(v.reshape(b, l, DT_PAD)[:, :, :N_DIR * M2_HEADS], 1, 2)
    mf, mb = _ssd(xact, pcs.reshape(b, l, DT_PAD), heads_on_sublanes(pcs), heads_on_sublanes(dt))
    t = b * l
    x1 = _mix(x2d, gates, u, s5f.reshape(t, -1), s5b.reshape(t, -1), xact.reshape(t, -1),
              mf.reshape(t, -1), mb.reshape(t, -1), z, lp)
    return _mlp(x1, lp["norm2_w"], lp["w_up"], lp["w_down"], final_norm_w, final)


def _trunk(x, layers, final_norm_w):
    b, l, _ = x.shape
    x2d = x.reshape(b * l, D_MODEL)
    for i, lp in enumerate(layers):
        x2d = _layer(x2d, b, l, lp, final_norm_w, i == len(layers) - 1)
    return x2d.reshape(b, l, D_MODEL)


def _prepare_layers(norm1_w, w_in, lam_re, lam_im, log_dt, b_re, b_im, c_re, c_im, d_s5, w_glu,
                    b_glu, w_s5_out, conv_w, conv_b, dt_bias, a_log, d_m2, m2_norm_w, w_m2_out,
                    w_o, norm2_w, w_up, w_down):
    layers = []
    for i in range(norm1_w.shape[0]):
        layers.append({
            "norm1_w": norm1_w[i].reshape(1, -1),
            "w_in": _pad_w_in(w_in[i]),
            "s5": _s5_params(lam_re[i], lam_im[i], log_dt[i], b_re[i], b_im[i], c_re[i], c_im[i]),
            "d_s5": d_s5[i].reshape(1, -1),
            "w_glu": w_glu[i].astype(BF16),
            "b_glu": b_glu[i].reshape(1, -1),
            "w_s5_out": w_s5_out[i].astype(BF16),
            "conv_w": conv_w[i],
            "conv_b": conv_b[i].reshape(1, -1),
            "m2": _m2_params(dt_bias[i], a_log[i]),
            "d_m2": jnp.repeat(d_m2[i], M2_HEADDIM).reshape(1, -1),
            "m2_norm_w": m2_norm_w[i].reshape(1, -1),
            "w_m2_out": w_m2_out[i].astype(BF16),
            "w_o": w_o[i].astype(BF16),
            "norm2_w": norm2_w[i].reshape(1, -1),
            "w_up": w_up[i].astype(BF16),
            "w_down": w_down[i].astype(BF16),
        })
    return layers


def kernel(x_prompt, x_sample, norm1_w, w_in, lam_re, lam_im, log_dt, b_re, b_im, c_re, c_im, d_s5, w_glu, b_glu, w_s5_out, conv_w, conv_b, dt_bias, a_log, d_m2, m2_norm_w, w_m2_out, w_o, norm2_w, w_up, w_down, final_norm_w):
    layers = _prepare_layers(norm1_w, w_in, lam_re, lam_im, log_dt, b_re, b_im, c_re, c_im, d_s5,
                             w_glu, b_glu, w_s5_out, conv_w, conv_b, dt_bias, a_log, d_m2,
                             m2_norm_w, w_m2_out, w_o, norm2_w, w_up, w_down)
    fnw = final_norm_w.reshape(1, -1)
    return (_trunk(x_prompt, layers, fnw), _trunk(x_sample, layers, fnw))
```

```python
import functools

import jax
import jax.numpy as jnp
from jax import lax
from jax.experimental import pallas as pl
from jax.experimental.pallas import tpu as pltpu

F32 = jnp.float32
BF16 = jnp.bfloat16

D_MODEL = 1024
DEPTH = 2
N_DIR = 2
EPS = 1e-6
S5_WIDTH = 768
S5_GROUP = 16
S5_GROUPS = 48
S5_STATE = 64
S5_LANES = S5_GROUPS * S5_STATE
M2_INNER = 1536
M2_HEADDIM = 64
M2_HEADS = 24
M2_GROUPS = 4
M2_STATE = 128
M2_CONV = 4
M2_GN = M2_GROUPS * M2_STATE
M2_CONV_DIM = M2_INNER + 2 * M2_GN
HEADS_PER_GROUP = M2_HEADS // M2_GROUPS
D_FF = 4096

LANES = 128
SUBLANES = 8
VMEM_LIMIT_BYTES = 56 * 1024 * 1024

LOG2E = 1.4426950408889634
DT_PAD = LANES
S5_SLAB_GROUPS = LANES // S5_GROUP
S5_SLABS = S5_WIDTH // LANES
S5_SLAB_STATES = S5_SLAB_GROUPS * S5_STATE

TM_INPROJ = 256
TM_MIX = 256
TM_MLP = 512
S5_BLOCK = 16
S5_ROWS = 128
SSD_CHUNK = 128
CONV_COLS = 256
CONV_SUB = 128
N_CHUNK_COLS = 512


def _cparams(flags=None):
    return pltpu.CompilerParams(dimension_semantics=None, vmem_limit_bytes=VMEM_LIMIT_BYTES, flags=flags)


def _resident(shape):
    nd = len(shape)
    return pl.BlockSpec(shape, lambda *_: (0,) * nd, pipeline_mode=pl.Buffered(1))


def _silu(x):
    return x * (1.0 / (1.0 + jnp.exp(-x)))


def _sigmoid(x):
    return 1.0 / (1.0 + jnp.exp(-x))


def _softplus(x):
    return jnp.maximum(x, 0.0) + jnp.log1p(jnp.exp(-jnp.abs(x)))


def _rms(x, w):
    var = jnp.mean(x * x, axis=-1, keepdims=True)
    return x * lax.rsqrt(var + EPS) * w


def _dot(a, b):
    return jnp.dot(a, b, preferred_element_type=F32)


def _split3(x):
    hi = x.astype(BF16)
    r1 = x - hi.astype(F32)
    mid = r1.astype(BF16)
    lo = (r1 - mid.astype(F32)).astype(BF16)
    return hi, mid, lo


def _conv_silu_rows(xe, tm, first, last, w, b):
    before = jnp.where(first, 0.0, xe[tm:tm + SUBLANES])
    after = jnp.where(last, 0.0, xe[tm + SUBLANES:tm + 2 * SUBLANES])
    ext = jnp.concatenate([before, xe[:tm], after], axis=0)
    outs = []
    for r0 in range(0, tm, CONV_SUB):
        blk = ext[r0:r0 + CONV_SUB + 2 * SUBLANES]
        nr = blk.shape[0]
        mid = slice(SUBLANES, SUBLANES + CONV_SUB)
        acc = (w[0:1] * pltpu.roll(blk, 1, 0)[mid] + w[1:2] * blk[mid] + w[2:3] * pltpu.roll(blk, nr - 1, 0)[mid]
               + w[3:4] * pltpu.roll(blk, nr - 2, 0)[mid] + b)
        outs.append(_silu(acc))
    return jnp.concatenate(outs, axis=0)


def _inproj_kernel(x_ref, xp_ref, xn_ref, nw_ref, w_ref, dtb_ref, alog_ref, cw_ref, cb_ref,
                   u_ref, z_ref, xact_ref, dt_ref, g_ref, pcs_ref, *, tiles_per_seq):
    tm = x_ref.shape[0]
    i = pl.program_id(0)
    first = lax.rem(i, tiles_per_seq) == 0
    last = lax.rem(i + 1, tiles_per_seq) == 0
    he = _rms(jnp.concatenate([x_ref[...], xp_ref[...], xn_ref[...]], axis=0), nw_ref[...]).astype(BF16)
    h = he[:tm]
    ident = lambda v: v
    posts = (ident, ident, None, lambda v: _softplus(v + dtb_ref[...]), _sigmoid)
    off = 0
    for ref, post in zip((u_ref, z_ref, xact_ref, dt_ref, g_ref), posts):
        n = ref.shape[-1]
        step = CONV_COLS if post is None else N_CHUNK_COLS
        for c0 in range(0, n, step):
            c1 = min(c0 + step, n)
            wc = w_ref[:, off + c0:off + c1]
            if post is None:
                val = _conv_silu_rows(_dot(he, wc), tm, first, last, cw_ref[:, c0:c1], cb_ref[:, c0:c1])
            else:
                val = post(_dot(h, wc))
            ref[:, c0:c1] = val.astype(ref.dtype)
        off += n
    q = SSD_CHUNK
    r = lax.broadcasted_iota(jnp.int32, (q, q), 0)
    c = lax.broadcasted_iota(jnp.int32, (q, q), 1)
    low = jnp.where(c <= r, 1.0, 0.0).astype(BF16)
    upp = jnp.where(c >= r, 1.0, 0.0).astype(BF16)
    fwd_col = lax.broadcasted_iota(jnp.int32, (q, DT_PAD), 1) < M2_HEADS
    for r0 in range(0, dt_ref.shape[0], q):
        parts = _split3(dt_ref[r0:r0 + q, :] * (-jnp.exp(alog_ref[...])))
        cs_f = sum(_dot(low, p) for p in parts)
        cs_b = sum(_dot(upp, p) for p in parts)
        pcs_ref[r0:r0 + q, :] = jnp.where(fwd_col, cs_f, cs_b) * LOG2E


def _inproj(x2d, seq_len, norm_w, w_in_p, dtb_tl, alog_tl, conv_w, conv_b):
    t = x2d.shape[0]
    tm = TM_INPROJ
    assert tm % SSD_CHUNK == 0 and seq_len % tm == 0
    widths = (S5_WIDTH, M2_INNER, M2_CONV_DIM, DT_PAD, 2 * D_MODEL, DT_PAD)
    dtypes = (BF16, BF16, BF16, F32, BF16, F32)
    row = lambda n: pl.BlockSpec((tm, n), lambda i: (i, 0))
    n8 = tm // SUBLANES
    before = pl.BlockSpec((SUBLANES, D_MODEL), lambda i: (jnp.maximum(i * n8 - 1, 0), 0))
    after = pl.BlockSpec((SUBLANES, D_MODEL), lambda i: (jnp.minimum((i + 1) * n8, t // SUBLANES - 1), 0))
    consts = (norm_w, w_in_p, dtb_tl, alog_tl, conv_w, conv_b)
    return pl.pallas_call(
        functools.partial(_inproj_kernel, tiles_per_seq=seq_len // tm),
        grid=(t // tm,),
        in_specs=[row(D_MODEL), before, after] + [_resident(w.shape) for w in consts],
        out_specs=[row(n) for n in widths],
        out_shape=[jax.ShapeDtypeStruct((t, n), dt) for n, dt in zip(widths, dtypes)],
        compiler_params=_cparams(),
        name="inproj",
    )(x2d, x2d, x2d, *consts)


def _s5_slab_halves():
    return [(k, h) for k in range(S5_SLABS) for h in range(2)]


S5_HALF = S5_BLOCK // 2
S5_GL = S5_BLOCK * S5_GROUP


def _s5_direction(ut_ref, we_ref, t0_ref, wy_ref, are_ref, aim_ref, ere_ref, eim_ref, car_ref, reverse, emit):
    nb = ut_ref.shape[1]
    ut = lambda lo, hi: ut_ref[0, :, lo:hi]

    @pl.when(pl.program_id(1) == 0)
    def _():
        car_ref[...] = jnp.zeros_like(car_ref)

    for gp in range(S5_GROUPS // 2):
        e = _dot(ut(gp * 2 * S5_GL, (gp + 1) * 2 * S5_GL), we_ref[gp])
        ere_ref[:, gp * LANES:(gp + 1) * LANES] = e[:, :LANES]
        eim_ref[:, gp * LANES:(gp + 1) * LANES] = e[:, LANES:]

    ar, ai = are_ref[...], aim_ref[...]

    def step(i, carry):
        sr, si = carry
        row = pl.ds(nb - 1 - i if reverse else i, 1)
        er, ei = ere_ref[row, :], eim_ref[row, :]
        ere_ref[row, :] = sr
        eim_ref[row, :] = si
        return ar * sr - ai * si + er, ar * si + ai * sr + ei

    sr, si = lax.fori_loop(0, nb, step, (car_ref[0:1, :], car_ref[1:2, :]), unroll=8)
    car_ref[0:1, :] = sr
    car_ref[1:2, :] = si

    for gp in range(S5_GROUPS // 2):
        sin = jnp.concatenate([ere_ref[:, gp * LANES:(gp + 1) * LANES], eim_ref[:, gp * LANES:(gp + 1) * LANES]],
                              axis=1).astype(BF16)
        yo = _dot(sin, wy_ref[gp])
        for gi in range(2):
            g = 2 * gp + gi
            emit(g, _dot(ut(g * S5_GL, (g + 1) * S5_GL), t0_ref[g]) + yo[:, gi * S5_GL:(gi + 1) * S5_GL])


def _s5_fwd_kernel(u_ref, p_ref, we_ref, t0_ref, wy_ref, are_ref, aim_ref, ut_ref, yt_ref,
                   stage_ref, ere_ref, eim_ref, car_ref):
    nb = u_ref.shape[1] // S5_BLOCK
    for k in range(S5_SLABS):
        stage_ref[k] = u_ref[0, :, k * LANES:(k + 1) * LANES].astype(F32)

    def offset_rows(k, r):
        return stage_ref[k, pl.ds(r, nb, stride=S5_BLOCK), :].astype(BF16)

    lhs = jnp.concatenate(
        [jnp.concatenate([offset_rows(k, S5_HALF * h + r) for r in range(S5_HALF)], axis=1)
         for k, h in _s5_slab_halves()], axis=0)
    perm = _dot(lhs, p_ref[...]).astype(BF16)
    for i, (k, h) in enumerate(_s5_slab_halves()):
        for gi in range(S5_SLAB_GROUPS):
            lo = (k * S5_SLAB_GROUPS + gi) * S5_GL + h * LANES
            ut_ref[0, :, lo:lo + LANES] = perm[i * nb:(i + 1) * nb, gi * LANES:(gi + 1) * LANES]

    def emit(g, yg):
        yt_ref[0, :, g * S5_GL:(g + 1) * S5_GL] = yg.astype(yt_ref.dtype)

    _s5_direction(ut_ref, we_ref, t0_ref, wy_ref, are_ref, aim_ref, ere_ref, eim_ref, car_ref, False, emit)


def _s5_bwd_kernel(ut_ref, ytf_ref, pt_ref, we_ref, t0_ref, wy_ref, are_ref, aim_ref, y_ref,
                   stage_ref, yt_ref, ere_ref, eim_ref, car_ref):
    nb = ut_ref.shape[1]

    def emit(g, yg):
        cols = slice(g * S5_GL, (g + 1) * S5_GL)
        yt_ref[:, cols] = (yg + ytf_ref[0, :, cols].astype(F32)).astype(yt_ref.dtype)

    _s5_direction(ut_ref, we_ref, t0_ref, wy_ref, are_ref, aim_ref, ere_ref, eim_ref, car_ref, True, emit)

    lhs = jnp.concatenate(
        [jnp.concatenate([yt_ref[:, (k * S5_SLAB_GROUPS + gi) * S5_GL + h * LANES:
                                 (k * S5_SLAB_GROUPS + gi) * S5_GL + (h + 1) * LANES]
                          for gi in range(S5_SLAB_GROUPS)], axis=1) for k, h in _s5_slab_halves()], axis=0)
    out = _dot(lhs, pt_ref[...])
    for i, (k, h) in enumerate(_s5_slab_halves()):
        for r in range(S5_HALF):
            stage_ref[k, pl.ds(S5_HALF * h + r, nb, stride=S5_BLOCK), :] = out[i * nb:(i + 1) * nb, r * LANES:(r + 1) * LANES]
    for k in range(S5_SLABS):
        y_ref[0, :, k * LANES:(k + 1) * LANES] = stage_ref[k].astype(y_ref.dtype)


def _s5_mixer_scan(u, sp):
    b, l, _ = u.shape
    nb = min(S5_ROWS, l // S5_BLOCK)
    nt = nb * S5_BLOCK
    nc = l // nt
    width = S5_BLOCK * S5_WIDTH
    tok_f = pl.BlockSpec((1, nt, S5_WIDTH), lambda i, c: (i, c, 0))
    blk_f = pl.BlockSpec((1, nb, width), lambda i, c: (i, c, 0))
    tok_b = pl.BlockSpec((1, nt, S5_WIDTH), lambda i, c: (i, nc - 1 - c, 0))
    blk_b = pl.BlockSpec((1, nb, width), lambda i, c: (i, nc - 1 - c, 0))
    blocked = jax.ShapeDtypeStruct((b, l // S5_BLOCK, width), BF16)
    state = [pltpu.VMEM((nb, S5_LANES), F32), pltpu.VMEM((nb, S5_LANES), F32), pltpu.VMEM((SUBLANES, S5_LANES), F32)]
    stage = pltpu.VMEM((S5_SLABS, nt, LANES), F32)
    f, r = sp
    wf = (f["perm"], f["we"], f["t0"], f["wy"], f["are"], f["aim"])
    ut, ytf = pl.pallas_call(
        _s5_fwd_kernel,
        grid=(b, nc),
        in_specs=[tok_f] + [_resident(w.shape) for w in wf],
        out_specs=[blk_f, blk_f],
        out_shape=[blocked, blocked],
        scratch_shapes=[stage] + state,
        compiler_params=_cparams(),
        name="s5fwd",
    )(u, *wf)
    wr = (r["perm_t"], r["we"], r["t0"], r["wy"], r["are"], r["aim"])
    return pl.pallas_call(
        _s5_bwd_kernel,
        grid=(b, nc),
        in_specs=[blk_b, blk_b] + [_resident(w.shape) for w in wr],
        out_specs=tok_b,
        out_shape=jax.ShapeDtypeStruct(u.shape, BF16),
        scratch_shapes=[stage, pltpu.VMEM((nb, width), BF16)] + state,
        compiler_params=_cparams(),
        name="s5bwd",
    )(ut, ytf, *wr)


def _ssd_chunk(xa, pcs_tl, pcs_hl2, dt_hl2, st_ref, d, reverse):
    q = xa.shape[0]
    hs = slice(d * M2_HEADS, (d + 1) * M2_HEADS)
    dt_hl = dt_hl2[hs, :]
    pcs_hl = pcs_hl2[hs, :]
    r = lax.broadcasted_iota(jnp.int32, (q, q), 0)
    c = lax.broadcasted_iota(jnp.int32, (q, q), 1)
    end = 0 if reverse else q - 1
    tot_hl = pcs_hl[:, end:end + 1]
    keep = (c >= r) if reverse else (c <= r)
    rowp_hl = pcs_hl - jnp.log2(dt_hl)
    wrow_hl = jnp.exp2(tot_hl - rowp_hl)
    dec_hl = jnp.exp2(tot_hl)

    lane = lax.broadcasted_iota(jnp.int32, (q, LANES), 1)
    first_head = lane < M2_HEADDIM
    first_head_row = lax.broadcasted_iota(jnp.int32, (1, LANES), 1) < M2_HEADDIM

    ys = []
    for g in range(M2_GROUPS):
        bm = xa[:, M2_INNER + g * M2_STATE:M2_INNER + (g + 1) * M2_STATE]
        cm = xa[:, M2_INNER + M2_GN + g * M2_STATE:M2_INNER + M2_GN + (g + 1) * M2_STATE]
        cb = lax.dot_general(cm, bm, (((1,), (1,)), ((), ())), preferred_element_type=F32)
        bt = bm.astype(F32).T
        st_g = st_ref[d, g]
        yoff = _dot(cm, st_g.astype(BF16))
        new_cols = []
        for jp in range(HEADS_PER_GROUP // 2):
            h0 = g * HEADS_PER_GROUP + 2 * jp
            pair = slice(h0 * M2_HEADDIM, (h0 + 2) * M2_HEADDIM)
            xp = xa[:, pair]
            xpf = xp.astype(F32)
            xbd = jnp.concatenate([jnp.where(first_head, xpf, 0.0), jnp.where(first_head, 0.0, xpf)],
                                  axis=0).astype(BF16)
            ms, ws, cols = [], [], []
            for h in (h0, h0 + 1):
                col = jnp.broadcast_to(pcs_tl[:, d * M2_HEADS + h:d * M2_HEADS + h + 1], (q, LANES))
                decay_dt = jnp.where(keep, jnp.exp2(jnp.tile(col, (1, q // LANES)) - rowp_hl[h:h + 1, :]), 0.0)
                ms.append((cb * decay_dt).astype(BF16))
                ws.append((bt * wrow_hl[h:h + 1, :]).astype(BF16))
                cols.append(col)
            m2 = jnp.concatenate(ms, axis=1)
            w2 = jnp.concatenate(ws, axis=1)
            yslab = yoff[:, 2 * jp * M2_HEADDIM:(2 * jp + 2) * M2_HEADDIM]
            ys.append(_dot(m2, xbd) + jnp.exp2(jnp.where(first_head, cols[0], cols[1])) * yslab)
            sslab = st_g[:, 2 * jp * M2_HEADDIM:(2 * jp + 2) * M2_HEADDIM]
            dec_row = jnp.where(first_head_row, dec_hl[h0:h0 + 1, :], dec_hl[h0 + 1:h0 + 2, :])
            new_cols.append(dec_row * sslab + _dot(w2, xbd))
        st_ref[d, g] = jnp.concatenate(new_cols, axis=1)
    return jnp.concatenate(ys, axis=1)


def _ssd_kernel(xf_ref, xb_ref, pf_ref, pb_ref, phf_ref, phb_ref, dthf_ref, dthb_ref, yf_ref, yb_ref, st_ref):
    @pl.when(pl.program_id(1) == 0)
    def _():
        st_ref[...] = jnp.zeros_like(st_ref)

    yf_ref[0] = _ssd_chunk(xf_ref[0], pf_ref[0], phf_ref[0], dthf_ref[0], st_ref, 0, False).astype(yf_ref.dtype)
    yb_ref[0] = _ssd_chunk(xb_ref[0], pb_ref[0], phb_ref[0], dthb_ref[0], st_ref, 1, True).astype(yb_ref.dtype)


def _ssd(xact, pcs_tl, pcs_hl, dt_hl):
    b, l, _ = xact.shape
    q = SSD_CHUNK
    nc = l // q
    fwd3 = lambda n: pl.BlockSpec((1, q, n), lambda i, c: (i, c, 0))
    bwd3 = lambda n: pl.BlockSpec((1, q, n), lambda i, c: (i, nc - 1 - c, 0))
    hl_f = pl.BlockSpec((1, N_DIR * M2_HEADS, q), lambda i, c: (i, 0, c))
    hl_b = pl.BlockSpec((1, N_DIR * M2_HEADS, q), lambda i, c: (i, 0, nc - 1 - c))
    return pl.pallas_call(
        _ssd_kernel,
        grid=(b, nc),
        in_specs=[fwd3(M2_CONV_DIM), bwd3(M2_CONV_DIM), fwd3(DT_PAD), bwd3(DT_PAD), hl_f, hl_b, hl_f, hl_b],
        out_specs=[fwd3(M2_INNER), bwd3(M2_INNER)],
        out_shape=[jax.ShapeDtypeStruct((b, l, M2_INNER), BF16)] * 2,
        scratch_shapes=[pltpu.VMEM((N_DIR, M2_GROUPS, M2_STATE, HEADS_PER_GROUP * M2_HEADDIM), F32)],
        compiler_params=_cparams(),
        name="ssd",
    )(xact, xact, pcs_tl, pcs_tl, pcs_hl, pcs_hl, dt_hl, dt_hl)


def _mix_kernel(x_ref, g_ref, u_ref, s5y_ref, xs_ref, mf_ref, mb_ref, z_ref,
                ds5_ref, wglu_ref, bglu_ref, ws5_ref, dm2_ref, nm2_ref, wm2_ref, wo_ref, o_ref):
    f = lambda ref: ref[...].astype(F32)
    y = ds5_ref[...] * f(u_ref) + f(s5y_ref)
    h = jax.nn.gelu(y)
    h = h * _sigmoid(_dot(h.astype(BF16), wglu_ref[...]) + bglu_ref[...])
    s5 = _dot(h.astype(BF16), ws5_ref[...])
    ym = dm2_ref[...] * f(xs_ref) + f(mf_ref) + f(mb_ref)
    gz = ym * _silu(f(z_ref))
    gw = M2_INNER // M2_GROUPS
    parts = []
    for i in range(M2_GROUPS):
        gs = gz[:, i * gw:(i + 1) * gw]
        parts.append(gs * lax.rsqrt(jnp.mean(gs * gs, axis=-1, keepdims=True) + EPS))
    gn = jnp.concatenate(parts, axis=1) * nm2_ref[...]
    m2 = _dot(gn.astype(BF16), wm2_ref[...])
    gates = f(g_ref)
    merged = gates[:, :D_MODEL] * s5 + gates[:, D_MODEL:] * m2
    o_ref[...] = x_ref[...] + _dot(merged.astype(BF16), wo_ref[...])


def _mix(x2d, gates, u, s5y, xact, mf, mb, z, mp):
    t = x2d.shape[0]
    tm = TM_MIX
    row = lambda n: pl.BlockSpec((tm, n), lambda i: (i, 0))
    weights = (mp["d_s5"], mp["w_glu"], mp["b_glu"], mp["w_s5_out"], mp["d_m2"], mp["m2_norm_w"],
               mp["w_m2_out"], mp["w_o"])
    return pl.pallas_call(
        _mix_kernel,
        grid=(t // tm,),
        in_specs=[row(D_MODEL), row(2 * D_MODEL), row(S5_WIDTH), row(S5_WIDTH),
                  row(M2_INNER), row(M2_INNER), row(M2_INNER), row(M2_INNER)]
        + [_resident(w.shape) for w in weights],
        out_specs=row(D_MODEL),
        out_shape=jax.ShapeDtypeStruct((t, D_MODEL), F32),
        compiler_params=_cparams(),
        name="mix",
    )(x2d, gates, u, s5y, xact, mf, mb, z, *weights)


def _mlp_kernel(x_ref, n2_ref, wup_ref, wdn_ref, fn_ref, o_ref, *, final):
    x = x_ref[...]
    h = _rms(x, n2_ref[...]).astype(BF16)
    acc = x
    for c0 in range(0, D_FF, 2 * N_CHUNK_COLS):
        c1 = c0 + 2 * N_CHUNK_COLS
        hid = jnp.maximum(_dot(h, wup_ref[:, c0:c1]), 0.0)
        acc = acc + _dot((hid * hid).astype(BF16), wdn_ref[c0:c1, :])
    o_ref[...] = _rms(acc, fn_ref[...]) if final else acc


def _mlp(x2d, norm2_w, w_up, w_down, final_norm_w, final):
    t = x2d.shape[0]
    tm = TM_MLP
    row = pl.BlockSpec((tm, D_MODEL), lambda i: (i, 0))
    weights = (norm2_w, w_up, w_down, final_norm_w)
    return pl.pallas_call(
        functools.partial(_mlp_kernel, final=final),
        grid=(t // tm,),
        in_specs=[row] + [_resident(w.shape) for w in weights],
        out_specs=row,
        out_shape=jax.ShapeDtypeStruct((t, D_MODEL), F32),
        compiler_params=_cparams(),
        name="mlp",
    )(x2d, *weights)


def _s5_perm():
    n = S5_SLAB_GROUPS * LANES
    i = jnp.arange(n)
    r_lo, gi, c = i // LANES, (i % LANES) // S5_GROUP, i % S5_GROUP
    dst = gi * LANES + r_lo * S5_GROUP + c
    return (dst[:, None] == jnp.arange(n)[None, :]).astype(BF16)


def _pair_rows_cols(m_re, m_im, rows_are_states):
    g, a, b = m_re.shape
    z = jnp.zeros_like(m_re[0::2])
    if rows_are_states:
        top = jnp.concatenate([m_re[0::2], z], axis=2), jnp.concatenate([z, m_re[1::2]], axis=2)
        bot = jnp.concatenate([m_im[0::2], z], axis=2), jnp.concatenate([z, m_im[1::2]], axis=2)
        return jnp.concatenate([top[0], top[1], bot[0], bot[1]], axis=1)
    left = jnp.concatenate([m_re[0::2], z, m_im[0::2], z], axis=2)
    right = jnp.concatenate([z, m_re[1::2], z, m_im[1::2]], axis=2)
    return jnp.concatenate([left, right], axis=1)


def _s5_params(lam_re, lam_im, log_dt, b_re, b_im, c_re, c_im):
    rb = S5_BLOCK
    delta = jnp.exp(log_dt)[..., None]
    mag = jnp.exp(lam_re * delta)
    a_re = mag * jnp.cos(lam_im * delta)
    a_im = mag * jnp.sin(lam_im * delta)
    inv = 1.0 / (lam_re * lam_re + lam_im * lam_im)
    q_re = ((a_re - 1.0) * lam_re + a_im * lam_im) * inv
    q_im = (a_im * lam_re - (a_re - 1.0) * lam_im) * inv
    bbar_re = q_re[..., None] * b_re - q_im[..., None] * b_im
    bbar_im = q_re[..., None] * b_im + q_im[..., None] * b_re
    kk = jnp.arange(rb + 1, dtype=F32).reshape(-1, 1, 1, 1)
    pk_mag = jnp.exp(kk * (lam_re * delta))
    pk_re = pk_mag * jnp.cos(kk * (lam_im * delta))
    pk_im = pk_mag * jnp.sin(kk * (lam_im * delta))
    perm = _s5_perm()
    r = jnp.arange(rb)
    out = []
    for d in range(N_DIR):
        fwd = d == 0
        pr, pi = pk_re[:, d], pk_im[:, d]
        w_re = pr[..., None] * bbar_re[d] - pi[..., None] * bbar_im[d]
        w_im = pr[..., None] * bbar_im[d] + pi[..., None] * bbar_re[d]
        k3 = kk[:rb, 0]
        ni_mag = jnp.exp(-k3 * (lam_re[d] * delta[d]))
        ni_re = ni_mag * jnp.cos(k3 * (lam_im[d] * delta[d]))
        ni_im = -ni_mag * jnp.sin(k3 * (lam_im[d] * delta[d]))
        src_re = ni_re[..., None] * bbar_re[d] - ni_im[..., None] * bbar_im[d]
        src_im = ni_re[..., None] * bbar_im[d] + ni_im[..., None] * bbar_re[d]
        dst_re = c_re[d][None] * pr[:rb, :, None, :] - c_im[d][None] * pi[:rb, :, None, :]
        dst_im = c_re[d][None] * pi[:rb, :, None, :] + c_im[d][None] * pr[:rb, :, None, :]
        if not fwd:
            src_re, src_im = w_re[:rb], w_im[:rb]
            dst_re = c_re[d][None] * ni_re[:, :, None, :] - c_im[d][None] * ni_im[:, :, None, :]
            dst_im = c_re[d][None] * ni_im[:, :, None, :] + c_im[d][None] * ni_re[:, :, None, :]
        n = rb * S5_GROUP
        src = jnp.transpose(jnp.concatenate([src_re, src_im], axis=2), (1, 0, 3, 2)).reshape(S5_GROUPS, n, -1)
        dst = jnp.transpose(jnp.concatenate([dst_re, -dst_im], axis=3), (1, 3, 0, 2)).reshape(S5_GROUPS, -1, n)
        t0 = jnp.matmul(src, dst, precision=lax.Precision.HIGHEST)
        causal = (r[None, :] >= r[:, None]) if fwd else (r[None, :] <= r[:, None])
        t0 = jnp.where(jnp.repeat(jnp.repeat(causal, S5_GROUP, axis=0), S5_GROUP, axis=1)[None], t0, 0.0)
        ke = (rb - 1 - r) if fwd else r
        we_re = jnp.transpose(w_re[ke], (1, 0, 3, 2)).reshape(S5_GROUPS, rb * S5_GROUP, S5_STATE)
        we_im = jnp.transpose(w_im[ke], (1, 0, 3, 2)).reshape(S5_GROUPS, rb * S5_GROUP, S5_STATE)
        ky = (r + 1) if fwd else (rb - r)
        g_re = c_re[d][None] * pr[ky][:, :, None, :] - c_im[d][None] * pi[ky][:, :, None, :]
        g_im = c_re[d][None] * pi[ky][:, :, None, :] + c_im[d][None] * pr[ky][:, :, None, :]
        wy_re = jnp.transpose(g_re, (1, 3, 0, 2)).reshape(S5_GROUPS, S5_STATE, rb * S5_GROUP)
        wy_im = jnp.transpose(-g_im, (1, 3, 0, 2)).reshape(S5_GROUPS, S5_STATE, rb * S5_GROUP)
        out.append({
            "perm": perm, "perm_t": perm.T,
            "we": _pair_rows_cols(we_re, we_im, False).astype(BF16),
            "t0": t0.astype(BF16),
            "wy": _pair_rows_cols(wy_re, wy_im, True).astype(BF16),
            "are": pr[rb].reshape(1, S5_LANES), "aim": pi[rb].reshape(1, S5_LANES),
        })
    return out


def _m2_params(dt_bias, a_log):
    pad = DT_PAD - N_DIR * M2_HEADS
    flat = lambda v: v.reshape(1, N_DIR * M2_HEADS)
    return {
        "dtb_tl": jnp.pad(flat(dt_bias), ((0, 0), (0, pad))),
        "alog_tl": jnp.pad(flat(a_log), ((0, 0), (0, pad))),
    }


def _pad_w_in(w_in):
    o_z = S5_WIDTH
    o_xbc = o_z + M2_INNER
    o_dt = o_xbc + M2_CONV_DIM
    o_g = o_dt + N_DIR * M2_HEADS
    dt = jnp.pad(w_in[:, o_dt:o_g], ((0, 0), (0, DT_PAD - N_DIR * M2_HEADS)))
    return jnp.concatenate([w_in[:, :o_dt], dt, w_in[:, o_g:]], axis=1).astype(BF16)


def _layer(x2d, b, l, lp, final_norm_w, final):
    u, z, xact, dt, gates, pcs = _inproj(x2d, l, lp["norm1_w"], lp["w_in"], lp["m2"]["dtb_tl"],
                                         lp["m2"]["alog_tl"], lp["conv_w"], lp["conv_b"])
    s5y = _s5_mixer_scan(u.reshape(b, l, S5_WIDTH), lp["s5"])
    xact = xact.reshape(b, l, M2_CONV_DIM)
    heads_on_sublanes = lambda v: jnp.swapaxes(v.reshape(b, l, DT_PAD)[:, :, :N_DIR * M2_HEADS], 1, 2)
    mf, mb = _ssd(xact, pcs.reshape(b, l, DT_PAD), heads_on_sublanes(pcs), heads_on_sublanes(dt))
    t = b * l
    x1 = _mix(x2d, gates, u, s5y.reshape(t, -1), xact.reshape(t, -1), mf.reshape(t, -1), mb.reshape(t, -1), z, lp)
    return _mlp(x1, lp["norm2_w"], lp["w_up"], lp["w_down"], final_norm_w, final)


def _trunk(x, layers, final_norm_w):
    b, l, _ = x.shape
    x2d = x.reshape(b * l, D_MODEL)
    for i, lp in enumerate(layers):
        x2d = _layer(x2d, b, l, lp, final_norm_w, i == len(layers) - 1)
    return x2d.reshape(b, l, D_MODEL)


def _prepare_layers(norm1_w, w_in, lam_re, lam_im, log_dt, b_re, b_im, c_re, c_im, d_s5, w_glu,
                    b_glu, w_s5_out, conv_w, conv_b, dt_bias, a_log, d_m2, m2_norm_w, w_m2_out,
                    w_o, norm2_w, w_up, w_down):
    layers = []
    for i in range(norm1_w.shape[0]):
        layers.append({
            "norm1_w": norm1_w[i].reshape(1, -1),
            "w_in": _pad_w_in(w_in[i]),
            "s5": _s5_params(lam_re[i], lam_im[i], log_dt[i], b_re[i], b_im[i], c_re[i], c_im[i]),
            "d_s5": d_s5[i].reshape(1, -1),
            "w_glu": w_glu[i].astype(BF16),
            "b_glu": b_glu[i].reshape(1, -1),
            "w_s5_out": w_s5_out[i].astype(BF16),
            "conv_w": conv_w[i],
            "conv_b": conv_b[i].reshape(1, -1),
            "m2": _m2_params(dt_bias[i], a_log[i]),
            "d_m2": jnp.repeat(d_m2[i], M2_HEADDIM).reshape(1, -1),
            "m2_norm_w": m2_norm_w[i].reshape(1, -1),
            "w_m2_out": w_m2_out[i].astype(BF16),
            "w_o": w_o[i].astype(BF16),
            "norm2_w": norm2_w[i].reshape(1, -1),
            "w_up": w_up[i].astype(BF16),
            "w_down": w_down[i].astype(BF16),
        })
    return layers


def kernel(x_prompt, x_sample, norm1_w, w_in, lam_re, lam_im, log_dt, b_re, b_im, c_re, c_im, d_s5, w_glu, b_glu, w_s5_out, conv_w, conv_b, dt_bias, a_log, d_m2, m2_norm_w, w_m2_out, w_o, norm2_w, w_up, w_down, final_norm_w):
    layers = _prepare_layers(norm1_w, w_in, lam_re, lam_im, log_dt, b_re, b_im, c_re, c_im, d_s5,
                             w_glu, b_glu, w_s5_out, conv_w, conv_b, dt_bias, a_log, d_m2,
                             m2_norm_w, w_m2_out, w_o, norm2_w, w_up, w_down)
    fnw = final_norm_w.reshape(1, -1)
    return (_trunk(x_prompt, layers, fnw), _trunk(x_sample, layers, fnw))
```

```python
import functools

import jax
import jax.numpy as jnp
from jax import lax
from jax.experimental import pallas as pl
from jax.experimental.pallas import tpu as pltpu

F32 = jnp.float32
BF16 = jnp.bfloat16

D_MODEL = 1024
DEPTH = 2
N_DIR = 2
EPS = 1e-6
S5_WIDTH = 768
S5_GROUP = 16
S5_GROUPS = 48
S5_STATE = 64
S5_LANES = S5_GROUPS * S5_STATE
M2_INNER = 1536
M2_HEADDIM = 64
M2_HEADS = 24
M2_GROUPS = 4
M2_STATE = 128
M2_CONV = 4
M2_GN = M2_GROUPS * M2_STATE
M2_CONV_DIM = M2_INNER + 2 * M2_GN
HEADS_PER_GROUP = M2_HEADS // M2_GROUPS
D_FF = 4096

LANES = 128
SUBLANES = 8
VMEM_LIMIT_BYTES = 56 * 1024 * 1024

LOG2E = 1.4426950408889634
DT_PAD = LANES
S5_SLAB_GROUPS = LANES // S5_GROUP
S5_SLABS = S5_WIDTH // LANES
S5_SLAB_STATES = S5_SLAB_GROUPS * S5_STATE

TM_INPROJ = 256
TM_MIX = 256
TM_MLP = 512
S5_BLOCK = 16
S5_ROWS = 128
SSD_CHUNK = 128
SSD_STEP_CHUNKS = 4
CONV_COLS = 256
CONV_SUB = 128
N_CHUNK_COLS = 512


def _cparams(flags=None):
    return pltpu.CompilerParams(dimension_semantics=None, vmem_limit_bytes=VMEM_LIMIT_BYTES, flags=flags)


def _resident(shape):
    nd = len(shape)
    return pl.BlockSpec(shape, lambda *_: (0,) * nd, pipeline_mode=pl.Buffered(1))


def _silu(x):
    return x * (1.0 / (1.0 + jnp.exp(-x)))


def _sigmoid(x):
    return 1.0 / (1.0 + jnp.exp(-x))


def _softplus(x):
    return jnp.maximum(x, 0.0) + jnp.log1p(jnp.exp(-jnp.abs(x)))


def _rms(x, w):
    var = jnp.mean(x * x, axis=-1, keepdims=True)
    return x * lax.rsqrt(var + EPS) * w


def _dot(a, b):
    return jnp.dot(a, b, preferred_element_type=F32)


def _split3(x):
    hi = x.astype(BF16)
    r1 = x - hi.astype(F32)
    mid = r1.astype(BF16)
    lo = (r1 - mid.astype(F32)).astype(BF16)
    return hi, mid, lo


def _conv_silu_rows(xe, tm, first, last, w, b):
    before = jnp.where(first, 0.0, xe[tm:tm + SUBLANES])
    after = jnp.where(last, 0.0, xe[tm + SUBLANES:tm + 2 * SUBLANES])
    ext = jnp.concatenate([before, xe[:tm], after], axis=0)
    outs = []
    for r0 in range(0, tm, CONV_SUB):
        blk = ext[r0:r0 + CONV_SUB + 2 * SUBLANES]
        nr = blk.shape[0]
        mid = slice(SUBLANES, SUBLANES + CONV_SUB)
        acc = (w[0:1] * pltpu.roll(blk, 1, 0)[mid] + w[1:2] * blk[mid] + w[2:3] * pltpu.roll(blk, nr - 1, 0)[mid]
               + w[3:4] * pltpu.roll(blk, nr - 2, 0)[mid] + b)
        outs.append(_silu(acc))
    return jnp.concatenate(outs, axis=0)


def _inproj_kernel(x_ref, xp_ref, xn_ref, nw_ref, w_ref, dtb_ref, alog_ref, cw_ref, cb_ref,
                   u_ref, z_ref, xact_ref, dt_ref, g_ref, pcs_ref, *, tiles_per_seq):
    tm = x_ref.shape[0]
    i = pl.program_id(0)
    first = lax.rem(i, tiles_per_seq) == 0
    last = lax.rem(i + 1, tiles_per_seq) == 0
    he = _rms(jnp.concatenate([x_ref[...], xp_ref[...], xn_ref[...]], axis=0), nw_ref[...]).astype(BF16)
    h = he[:tm]
    ident = lambda v: v
    posts = (ident, ident, None, lambda v: _softplus(v + dtb_ref[...]), _sigmoid)
    off = 0
    for ref, post in zip((u_ref, z_ref, xact_ref, dt_ref, g_ref), posts):
        n = ref.shape[-1]
        step = CONV_COLS if post is None else N_CHUNK_COLS
        for c0 in range(0, n, step):
            c1 = min(c0 + step, n)
            wc = w_ref[:, off + c0:off + c1]
            if post is None:
                val = _conv_silu_rows(_dot(he, wc), tm, first, last, cw_ref[:, c0:c1], cb_ref[:, c0:c1])
            else:
                val = post(_dot(h, wc))
            ref[:, c0:c1] = val.astype(ref.dtype)
        off += n
    q = SSD_CHUNK
    r = lax.broadcasted_iota(jnp.int32, (q, q), 0)
    c = lax.broadcasted_iota(jnp.int32, (q, q), 1)
    low = jnp.where(c <= r, 1.0, 0.0).astype(BF16)
    upp = jnp.where(c >= r, 1.0, 0.0).astype(BF16)
    fwd_col = lax.broadcasted_iota(jnp.int32, (q, DT_PAD), 1) < M2_HEADS
    for r0 in range(0, dt_ref.shape[0], q):
        parts = _split3(dt_ref[r0:r0 + q, :] * (-jnp.exp(alog_ref[...])))
        cs_f = sum(_dot(low, p) for p in parts)
        cs_b = sum(_dot(upp, p) for p in parts)
        pcs_ref[r0:r0 + q, :] = jnp.where(fwd_col, cs_f, cs_b) * LOG2E


def _inproj(x2d, seq_len, norm_w, w_in_p, dtb_tl, alog_tl, conv_w, conv_b):
    t = x2d.shape[0]
    tm = TM_INPROJ
    assert tm % SSD_CHUNK == 0 and seq_len % tm == 0
    widths = (S5_WIDTH, M2_INNER, M2_CONV_DIM, DT_PAD, 2 * D_MODEL, DT_PAD)
    dtypes = (BF16, BF16, BF16, F32, BF16, F32)
    row = lambda n: pl.BlockSpec((tm, n), lambda i: (i, 0))
    n8 = tm // SUBLANES
    before = pl.BlockSpec((SUBLANES, D_MODEL), lambda i: (jnp.maximum(i * n8 - 1, 0), 0))
    after = pl.BlockSpec((SUBLANES, D_MODEL), lambda i: (jnp.minimum((i + 1) * n8, t // SUBLANES - 1), 0))
    consts = (norm_w, w_in_p, dtb_tl, alog_tl, conv_w, conv_b)
    return pl.pallas_call(
        functools.partial(_inproj_kernel, tiles_per_seq=seq_len // tm),
        grid=(t // tm,),
        in_specs=[row(D_MODEL), before, after] + [_resident(w.shape) for w in consts],
        out_specs=[row(n) for n in widths],
        out_shape=[jax.ShapeDtypeStruct((t, n), dt) for n, dt in zip(widths, dtypes)],
        compiler_params=_cparams(),
        name="inproj",
    )(x2d, x2d, x2d, *consts)


def _s5_slab_halves():
    return [(k, h) for k in range(S5_SLABS) for h in range(2)]


S5_HALF = S5_BLOCK // 2
S5_GL = S5_BLOCK * S5_GROUP


def _s5_direction(ut_ref, we_ref, t0_ref, wy_ref, are_ref, aim_ref, ere_ref, eim_ref, car_ref, reverse, emit):
    nb = ut_ref.shape[1]
    ut = lambda lo, hi: ut_ref[0, :, lo:hi]

    @pl.when(pl.program_id(1) == 0)
    def _():
        car_ref[...] = jnp.zeros_like(car_ref)

    for gp in range(S5_GROUPS // 2):
        e = _dot(ut(gp * 2 * S5_GL, (gp + 1) * 2 * S5_GL), we_ref[gp])
        ere_ref[:, gp * LANES:(gp + 1) * LANES] = e[:, :LANES]
        eim_ref[:, gp * LANES:(gp + 1) * LANES] = e[:, LANES:]

    ar, ai = are_ref[...], aim_ref[...]

    def step(i, carry):
        sr, si = carry
        row = pl.ds(nb - 1 - i if reverse else i, 1)
        er, ei = ere_ref[row, :], eim_ref[row, :]
        ere_ref[row, :] = sr
        eim_ref[row, :] = si
        return ar * sr - ai * si + er, ar * si + ai * sr + ei

    sr, si = lax.fori_loop(0, nb, step, (car_ref[0:1, :], car_ref[1:2, :]), unroll=8)
    car_ref[0:1, :] = sr
    car_ref[1:2, :] = si

    for gp in range(S5_GROUPS // 2):
        sin = jnp.concatenate([ere_ref[:, gp * LANES:(gp + 1) * LANES], eim_ref[:, gp * LANES:(gp + 1) * LANES]],
                              axis=1).astype(BF16)
        yo = _dot(sin, wy_ref[gp])
        for gi in range(2):
            g = 2 * gp + gi
            emit(g, _dot(ut(g * S5_GL, (g + 1) * S5_GL), t0_ref[g]) + yo[:, gi * S5_GL:(gi + 1) * S5_GL])


def _s5_fwd_kernel(u_ref, p_ref, we_ref, t0_ref, wy_ref, are_ref, aim_ref, ut_ref, yt_ref,
                   stage_ref, ere_ref, eim_ref, car_ref):
    nb = u_ref.shape[1] // S5_BLOCK
    for k in range(S5_SLABS):
        stage_ref[k] = u_ref[0, :, k * LANES:(k + 1) * LANES].astype(F32)

    def offset_rows(k, r):
        return stage_ref[k, pl.ds(r, nb, stride=S5_BLOCK), :].astype(BF16)

    lhs = jnp.concatenate(
        [jnp.concatenate([offset_rows(k, S5_HALF * h + r) for r in range(S5_HALF)], axis=1)
         for k, h in _s5_slab_halves()], axis=0)
    perm = _dot(lhs, p_ref[...]).astype(BF16)
    for i, (k, h) in enumerate(_s5_slab_halves()):
        for gi in range(S5_SLAB_GROUPS):
            lo = (k * S5_SLAB_GROUPS + gi) * S5_GL + h * LANES
            ut_ref[0, :, lo:lo + LANES] = perm[i * nb:(i + 1) * nb, gi * LANES:(gi + 1) * LANES]

    def emit(g, yg):
        yt_ref[0, :, g * S5_GL:(g + 1) * S5_GL] = yg.astype(yt_ref.dtype)

    _s5_direction(ut_ref, we_ref, t0_ref, wy_ref, are_ref, aim_ref, ere_ref, eim_ref, car_ref, False, emit)


def _s5_bwd_kernel(ut_ref, ytf_ref, pt_ref, we_ref, t0_ref, wy_ref, are_ref, aim_ref, y_ref,
                   stage_ref, yt_ref, ere_ref, eim_ref, car_ref):
    nb = ut_ref.shape[1]

    def emit(g, yg):
        cols = slice(g * S5_GL, (g + 1) * S5_GL)
        yt_ref[:, cols] = (yg + ytf_ref[0, :, cols].astype(F32)).astype(yt_ref.dtype)

    _s5_direction(ut_ref, we_ref, t0_ref, wy_ref, are_ref, aim_ref, ere_ref, eim_ref, car_ref, True, emit)

    lhs = jnp.concatenate(
        [jnp.concatenate([yt_ref[:, (k * S5_SLAB_GROUPS + gi) * S5_GL + h * LANES:
                                 (k * S5_SLAB_GROUPS + gi) * S5_GL + (h + 1) * LANES]
                          for gi in range(S5_SLAB_GROUPS)], axis=1) for k, h in _s5_slab_halves()], axis=0)
    out = _dot(lhs, pt_ref[...])
    for i, (k, h) in enumerate(_s5_slab_halves()):
        for r in range(S5_HALF):
            stage_ref[k, pl.ds(S5_HALF * h + r, nb, stride=S5_BLOCK), :] = out[i * nb:(i + 1) * nb, r * LANES:(r + 1) * LANES]
    for k in range(S5_SLABS):
        y_ref[0, :, k * LANES:(k + 1) * LANES] = stage_ref[k].astype(y_ref.dtype)


def _s5_mixer_scan(u, sp):
    b, l, _ = u.shape
    nb = min(S5_ROWS, l // S5_BLOCK)
    nt = nb * S5_BLOCK
    nc = l // nt
    width = S5_BLOCK * S5_WIDTH
    tok_f = pl.BlockSpec((1, nt, S5_WIDTH), lambda i, c: (i, c, 0))
    blk_f = pl.BlockSpec((1, nb, width), lambda i, c: (i, c, 0))
    tok_b = pl.BlockSpec((1, nt, S5_WIDTH), lambda i, c: (i, nc - 1 - c, 0))
    blk_b = pl.BlockSpec((1, nb, width), lambda i, c: (i, nc - 1 - c, 0))
    blocked = jax.ShapeDtypeStruct((b, l // S5_BLOCK, width), BF16)
    state = [pltpu.VMEM((nb, S5_LANES), F32), pltpu.VMEM((nb, S5_LANES), F32), pltpu.VMEM((SUBLANES, S5_LANES), F32)]
    stage = pltpu.VMEM((S5_SLABS, nt, LANES), F32)
    f, r = sp
    wf = (f["perm"], f["we"], f["t0"], f["wy"], f["are"], f["aim"])
    ut, ytf = pl.pallas_call(
        _s5_fwd_kernel,
        grid=(b, nc),
        in_specs=[tok_f] + [_resident(w.shape) for w in wf],
        out_specs=[blk_f, blk_f],
        out_shape=[blocked, blocked],
        scratch_shapes=[stage] + state,
        compiler_params=_cparams(),
        name="s5fwd",
    )(u, *wf)
    wr = (r["perm_t"], r["we"], r["t0"], r["wy"], r["are"], r["aim"])
    return pl.pallas_call(
        _s5_bwd_kernel,
        grid=(b, nc),
        in_specs=[blk_b, blk_b] + [_resident(w.shape) for w in wr],
        out_specs=tok_b,
        out_shape=jax.ShapeDtypeStruct(u.shape, BF16),
        scratch_shapes=[stage, pltpu.VMEM((nb, width), BF16)] + state,
        compiler_params=_cparams(),
        name="s5bwd",
    )(ut, ytf, *wr)


def _ssd_chunk(xa, pcs_tl, pcs_hl2, dt_hl2, st_ref, d, reverse):
    q = xa.shape[0]
    hs = slice(d * M2_HEADS, (d + 1) * M2_HEADS)
    dt_hl = dt_hl2[hs, :]
    pcs_hl = pcs_hl2[hs, :]
    r = lax.broadcasted_iota(jnp.int32, (q, q), 0)
    c = lax.broadcasted_iota(jnp.int32, (q, q), 1)
    end = 0 if reverse else q - 1
    tot_hl = pcs_hl[:, end:end + 1]
    keep = (c >= r) if reverse else (c <= r)
    rowp_hl = pcs_hl - jnp.log2(dt_hl)
    wrow_hl = jnp.exp2(tot_hl - rowp_hl)
    dec_hl = jnp.exp2(tot_hl)

    lane = lax.broadcasted_iota(jnp.int32, (q, LANES), 1)
    first_head = lane < M2_HEADDIM
    first_head_row = lax.broadcasted_iota(jnp.int32, (1, LANES), 1) < M2_HEADDIM

    ys = []
    for g in range(M2_GROUPS):
        bm = xa[:, M2_INNER + g * M2_STATE:M2_INNER + (g + 1) * M2_STATE]
        cm = xa[:, M2_INNER + M2_GN + g * M2_STATE:M2_INNER + M2_GN + (g + 1) * M2_STATE]
        cb = lax.dot_general(cm, bm, (((1,), (1,)), ((), ())), preferred_element_type=F32)
        bt = bm.astype(F32).T
        st_g = st_ref[d, g]
        yoff = _dot(cm, st_g.astype(BF16))
        new_cols = []
        for jp in range(HEADS_PER_GROUP // 2):
            h0 = g * HEADS_PER_GROUP + 2 * jp
            pair = slice(h0 * M2_HEADDIM, (h0 + 2) * M2_HEADDIM)
            xp = xa[:, pair]
            xpf = xp.astype(F32)
            xbd = jnp.concatenate([jnp.where(first_head, xpf, 0.0), jnp.where(first_head, 0.0, xpf)],
                                  axis=0).astype(BF16)
            ms, ws, cols = [], [], []
            for h in (h0, h0 + 1):
                col = jnp.broadcast_to(pcs_tl[:, d * M2_HEADS + h:d * M2_HEADS + h + 1], (q, LANES))
                decay_dt = jnp.where(keep, jnp.exp2(jnp.tile(col, (1, q // LANES)) - rowp_hl[h:h + 1, :]), 0.0)
                ms.append((cb * decay_dt).astype(BF16))
                ws.append((bt * wrow_hl[h:h + 1, :]).astype(BF16))
                cols.append(col)
            m2 = jnp.concatenate(ms, axis=1)
            w2 = jnp.concatenate(ws, axis=1)
            yslab = yoff[:, 2 * jp * M2_HEADDIM:(2 * jp + 2) * M2_HEADDIM]
            ys.append(_dot(m2, xbd) + jnp.exp2(jnp.where(first_head, cols[0], cols[1])) * yslab)
            sslab = st_g[:, 2 * jp * M2_HEADDIM:(2 * jp + 2) * M2_HEADDIM]
            dec_row = jnp.where(first_head_row, dec_hl[h0:h0 + 1, :], dec_hl[h0 + 1:h0 + 2, :])
            new_cols.append(dec_row * sslab + _dot(w2, xbd))
        st_ref[d, g] = jnp.concatenate(new_cols, axis=1)
    return jnp.concatenate(ys, axis=1)


def _ssd_kernel(xf_ref, xb_ref, pf_ref, pb_ref, phf_ref, phb_ref, dthf_ref, dthb_ref, yf_ref, yb_ref, st_ref):
    @pl.when(pl.program_id(1) == 0)
    def _():
        st_ref[...] = jnp.zeros_like(st_ref)

    q = SSD_CHUNK
    n_sub = xf_ref.shape[1] // q
    for j in range(n_sub):
        rf = slice(j * q, (j + 1) * q)
        rb = slice((n_sub - 1 - j) * q, (n_sub - j) * q)
        yf_ref[0, rf, :] = _ssd_chunk(xf_ref[0, rf, :], pf_ref[0, rf, :], phf_ref[0, :, rf], dthf_ref[0, :, rf],
                                      st_ref, 0, False).astype(yf_ref.dtype)
        yb_ref[0, rb, :] = _ssd_chunk(xb_ref[0, rb, :], pb_ref[0, rb, :], phb_ref[0, :, rb], dthb_ref[0, :, rb],
                                      st_ref, 1, True).astype(yb_ref.dtype)


def _ssd(xact, pcs_tl, pcs_hl, dt_hl):
    b, l, _ = xact.shape
    q = SSD_CHUNK * SSD_STEP_CHUNKS
    nc = l // q
    fwd3 = lambda n: pl.BlockSpec((1, q, n), lambda i, c: (i, c, 0))
    bwd3 = lambda n: pl.BlockSpec((1, q, n), lambda i, c: (i, nc - 1 - c, 0))
    hl_f = pl.BlockSpec((1, N_DIR * M2_HEADS, q), lambda i, c: (i, 0, c))
    hl_b = pl.BlockSpec((1, N_DIR * M2_HEADS, q), lambda i, c: (i, 0, nc - 1 - c))
    return pl.pallas_call(
        _ssd_kernel,
        grid=(b, nc),
        in_specs=[fwd3(M2_CONV_DIM), bwd3(M2_CONV_DIM), fwd3(DT_PAD), bwd3(DT_PAD), hl_f, hl_b, hl_f, hl_b],
        out_specs=[fwd3(M2_INNER), bwd3(M2_INNER)],
        out_shape=[jax.ShapeDtypeStruct((b, l, M2_INNER), BF16)] * 2,
        scratch_shapes=[pltpu.VMEM((N_DIR, M2_GROUPS, M2_STATE, HEADS_PER_GROUP * M2_HEADDIM), F32)],
        compiler_params=_cparams(),
        name="ssd",
    )(xact, xact, pcs_tl, pcs_tl, pcs_hl, pcs_hl, dt_hl, dt_hl)


def _mix_kernel(x_ref, g_ref, u_ref, s5y_ref, xs_ref, mf_ref, mb_ref, z_ref,
                ds5_ref, wglu_ref, bglu_ref, ws5_ref, dm2_ref, nm2_ref, wm2_ref, wo_ref, o_ref):
    f = lambda ref: ref[...].astype(F32)
    y = ds5_ref[...] * f(u_ref) + f(s5y_ref)
    h = jax.nn.gelu(y)
    h = h * _sigmoid(_dot(h.astype(BF16), wglu_ref[...]) + bglu_ref[...])
    s5 = _dot(h.astype(BF16), ws5_ref[...])
    ym = dm2_ref[...] * f(xs_ref) + f(mf_ref) + f(mb_ref)
    gz = ym * _silu(f(z_ref))
    gw = M2_INNER // M2_GROUPS
    parts = []
    for i in range(M2_GROUPS):
        gs = gz[:, i * gw:(i + 1) * gw]
        parts.append(gs * lax.rsqrt(jnp.mean(gs * gs, axis=-1, keepdims=True) + EPS))
    gn = jnp.concatenate(parts, axis=1) * nm2_ref[...]
    m2 = _dot(gn.astype(BF16), wm2_ref[...])
    gates = f(g_ref)
    merged = gates[:, :D_MODEL] * s5 + gates[:, D_MODEL:] * m2
    o_ref[...] = x_ref[...] + _dot(merged.astype(BF16), wo_ref[...])


def _mix(x2d, gates, u, s5y, xact, mf, mb, z, mp):
    t = x2d.shape[0]
    tm = TM_MIX
    row = lambda n: pl.BlockSpec((tm, n), lambda i: (i, 0))
    weights = (mp["d_s5"], mp["w_glu"], mp["b_glu"], mp["w_s5_out"], mp["d_m2"], mp["m2_norm_w"],
               mp["w_m2_out"], mp["w_o"])
    return pl.pallas_call(
        _mix_kernel,
        grid=(t // tm,),
        in_specs=[row(D_MODEL), row(2 * D_MODEL), row(S5_WIDTH), row(S5_WIDTH),
                  row(M2_INNER), row(M2_INNER), row(M2_INNER), row(M2_INNER)]
        + [_resident(w.shape) for w in weights],
        out_specs=row(D_MODEL),
        out_shape=jax.ShapeDtypeStruct((t, D_MODEL), F32),
        compiler_params=_cparams(),
        name="mix",
    )(x2d, gates, u, s5y, xact, mf, mb, z, *weights)


def _mlp_kernel(x_ref, n2_ref, wup_ref, wdn_ref, fn_ref, o_ref, *, final):
    x = x_ref[...]
    h = _rms(x, n2_ref[...]).astype(BF16)
    acc = x
    for c0 in range(0, D_FF, 2 * N_CHUNK_COLS):
        c1 = c0 + 2 * N_CHUNK_COLS
        hid = jnp.maximum(_dot(h, wup_ref[:, c0:c1]), 0.0)
        acc = acc + _dot((hid * hid).astype(BF16), wdn_ref[c0:c1, :])
    o_ref[...] = _rms(acc, fn_ref[...]) if final else acc


def _mlp(x2d, norm2_w, w_up, w_down, final_norm_w, final):
    t = x2d.shape[0]
    tm = TM_MLP
    row = pl.BlockSpec((tm, D_MODEL), lambda i: (i, 0))
    weights = (norm2_w, w_up, w_down, final_norm_w)
    return pl.pallas_call(
        functools.partial(_mlp_kernel, final=final),
        grid=(t // tm,),
        in_specs=[row] + [_resident(w.shape) for w in weights],
        out_specs=row,
        out_shape=jax.ShapeDtypeStruct((t, D_MODEL), F32),
        compiler_params=_cparams(),
        name="mlp",
    )(x2d, *weights)


def _s5_perm():
    n = S5_SLAB_GROUPS * LANES
    i = jnp.arange(n)
    r_lo, gi, c = i // LANES, (i % LANES) // S5_GROUP, i % S5_GROUP
    dst = gi * LANES + r_lo * S5_GROUP + c
    return (dst[:, None] == jnp.arange(n)[None, :]).astype(BF16)


def _pair_rows_cols(m_re, m_im, rows_are_states):
    g, a, b = m_re.shape
    z = jnp.zeros_like(m_re[0::2])
    if rows_are_states:
        top = jnp.concatenate([m_re[0::2], z], axis=2), jnp.concatenate([z, m_re[1::2]], axis=2)
        bot = jnp.concatenate([m_im[0::2], z], axis=2), jnp.concatenate([z, m_im[1::2]], axis=2)
        return jnp.concatenate([top[0], top[1], bot[0], bot[1]], axis=1)
    left = jnp.concatenate([m_re[0::2], z, m_im[0::2], z], axis=2)
    right = jnp.concatenate([z, m_re[1::2], z, m_im[1::2]], axis=2)
    return jnp.concatenate([left, right], axis=1)


def _s5_params(lam_re, lam_im, log_dt, b_re, b_im, c_re, c_im):
    rb = S5_BLOCK
    delta = jnp.exp(log_dt)[..., None]
    mag = jnp.exp(lam_re * delta)
    a_re = mag * jnp.cos(lam_im * delta)
    a_im = mag * jnp.sin(lam_im * delta)
    inv = 1.0 / (lam_re * lam_re + lam_im * lam_im)
    q_re = ((a_re - 1.0) * lam_re + a_im * lam_im) * inv
    q_im = (a_im * lam_re - (a_re - 1.0) * lam_im) * inv
    bbar_re = q_re[..., None] * b_re - q_im[..., None] * b_im
    bbar_im = q_re[..., None] * b_im + q_im[..., None] * b_re
    kk = jnp.arange(rb + 1, dtype=F32).reshape(-1, 1, 1, 1)
    pk_mag = jnp.exp(kk * (lam_re * delta))
    pk_re = pk_mag * jnp.cos(kk * (lam_im * delta))
    pk_im = pk_mag * jnp.sin(kk * (lam_im * delta))
    perm = _s5_perm()
    r = jnp.arange(rb)
    out = []
    for d in range(N_DIR):
        fwd = d == 0
        pr, pi = pk_re[:, d], pk_im[:, d]
        w_re = pr[..., None] * bbar_re[d] - pi[..., None] * bbar_im[d]
        w_im = pr[..., None] * bbar_im[d] + pi[..., None] * bbar_re[d]
        k3 = kk[:rb, 0]
        ni_mag = jnp.exp(-k3 * (lam_re[d] * delta[d]))
        ni_re = ni_mag * jnp.cos(k3 * (lam_im[d] * delta[d]))
        ni_im = -ni_mag * jnp.sin(k3 * (lam_im[d] * delta[d]))
        src_re = ni_re[..., None] * bbar_re[d] - ni_im[..., None] * bbar_im[d]
        src_im = ni_re[..., None] * bbar_im[d] + ni_im[..., None] * bbar_re[d]
        dst_re = c_re[d][None] * pr[:rb, :, None, :] - c_im[d][None] * pi[:rb, :, None, :]
        dst_im = c_re[d][None] * pi[:rb, :, None, :] + c_im[d][None] * pr[:rb, :, None, :]
        if not fwd:
            src_re, src_im = w_re[:rb], w_im[:rb]
            dst_re = c_re[d][None] * ni_re[:, :, None, :] - c_im[d][None] * ni_im[:, :, None, :]
            dst_im = c_re[d][None] * ni_im[:, :, None, :] + c_im[d][None] * ni_re[:, :, None, :]
        n = rb * S5_GROUP
        src = jnp.transpose(jnp.concatenate([src_re, src_im], axis=2), (1, 0, 3, 2)).reshape(S5_GROUPS, n, -1)
        dst = jnp.transpose(jnp.concatenate([dst_re, -dst_im], axis=3), (1, 3, 0, 2)).reshape(S5_GROUPS, -1, n)
        t0 = jnp.matmul(src, dst, precision=lax.Precision.HIGHEST)
        causal = (r[None, :] >= r[:, None]) if fwd else (r[None, :] <= r[:, None])
        t0 = jnp.where(jnp.repeat(jnp.repeat(causal, S5_GROUP, axis=0), S5_GROUP, axis=1)[None], t0, 0.0)
        ke = (rb - 1 - r) if fwd else r
        we_re = jnp.transpose(w_re[ke], (1, 0, 3, 2)).reshape(S5_GROUPS, rb * S5_GROUP, S5_STATE)
        we_im = jnp.transpose(w_im[ke], (1, 0, 3, 2)).reshape(S5_GROUPS, rb * S5_GROUP, S5_STATE)
        ky = (r + 1) if fwd else (rb - r)
        g_re = c_re[d][None] * pr[ky][:, :, None, :] - c_im[d][None] * pi[ky][:, :, None, :]
        g_im = c_re[d][None] * pi[ky][:, :, None, :] + c_im[d][None] * pr[ky][:, :, None, :]
        wy_re = jnp.transpose(g_re, (1, 3, 0, 2)).reshape(S5_GROUPS, S5_STATE, rb * S5_GROUP)
        wy_im = jnp.transpose(-g_im, (1, 3, 0, 2)).reshape(S5_GROUPS, S5_STATE, rb * S5_GROUP)
        out.append({
            "perm": perm, "perm_t": perm.T,
            "we": _pair_rows_cols(we_re, we_im, False).astype(BF16),
            "t0": t0.astype(BF16),
            "wy": _pair_rows_cols(wy_re, wy_im, True).astype(BF16),
            "are": pr[rb].reshape(1, S5_LANES), "aim": pi[rb].reshape(1, S5_LANES),
        })
    return out


def _m2_params(dt_bias, a_log):
    pad = DT_PAD - N_DIR * M2_HEADS
    flat = lambda v: v.reshape(1, N_DIR * M2_HEADS)
    return {
        "dtb_tl": jnp.pad(flat(dt_bias), ((0, 0), (0, pad))),
        "alog_tl": jnp.pad(flat(a_log), ((0, 0), (0, pad))),
    }


def _pad_w_in(w_in):
    o_z = S5_WIDTH
    o_xbc = o_z + M2_INNER
    o_dt = o_xbc + M2_CONV_DIM
    o_g = o_dt + N_DIR * M2_HEADS
    dt = jnp.pad(w_in[:, o_dt:o_g], ((0, 0), (0, DT_PAD - N_DIR * M2_HEADS)))
    return jnp.concatenate([w_in[:, :o_dt], dt, w_in[:, o_g:]], axis=1).astype(BF16)


def _layer(x2d, b, l, lp, final_norm_w, final):
    u, z, xact, dt, gates, pcs = _inproj(x2d, l, lp["norm1_w"], lp["w_in"], lp["m2"]["dtb_tl"],
                                         lp["m2"]["alog_tl"], lp["conv_w"], lp["conv_b"])
    s5y = _s5_mixer_scan(u.reshape(b, l, S5_WIDTH), lp["s5"])
    xact = xact.reshape(b, l, M2_CONV_DIM)
    heads_on_sublanes = lambda v: jnp.swapaxes(v.reshape(b, l, DT_PAD)[:, :, :N_DIR * M2_HEADS], 1, 2)
    mf, mb = _ssd(xact, pcs.reshape(b, l, DT_PAD), heads_on_sublanes(pcs), heads_on_sublanes(dt))
    t = b * l
    x1 = _mix(x2d, gates, u, s5y.reshape(t, -1), xact.reshape(t, -1), mf.reshape(t, -1), mb.reshape(t, -1), z, lp)
    return _mlp(x1, lp["norm2_w"], lp["w_up"], lp["w_down"], final_norm_w, final)


def _trunk(x, layers, final_norm_w):
    b, l, _ = x.shape
    x2d = x.reshape(b * l, D_MODEL)
    for i, lp in enumerate(layers):
        x2d = _layer(x2d, b, l, lp, final_norm_w, i == len(layers) - 1)
    return x2d.reshape(b, l, D_MODEL)


def _prepare_layers(norm1_w, w_in, lam_re, lam_im, log_dt, b_re, b_im, c_re, c_im, d_s5, w_glu,
                    b_glu, w_s5_out, conv_w, conv_b, dt_bias, a_log, d_m2, m2_norm_w, w_m2_out,
                    w_o, norm2_w, w_up, w_down):
    layers = []
    for i in range(norm1_w.shape[0]):
        layers.append({
            "norm1_w": norm1_w[i].reshape(1, -1),
            "w_in": _pad_w_in(w_in[i]),
            "s5": _s5_params(lam_re[i], lam_im[i], log_dt[i], b_re[i], b_im[i], c_re[i], c_im[i]),
            "d_s5": d_s5[i].reshape(1, -1),
            "w_glu": w_glu[i].astype(BF16),
            "b_glu": b_glu[i].reshape(1, -1),
            "w_s5_out": w_s5_out[i].astype(BF16),
            "conv_w": conv_w[i],
            "conv_b": conv_b[i].reshape(1, -1),
            "m2": _m2_params(dt_bias[i], a_log[i]),
            "d_m2": jnp.repeat(d_m2[i], M2_HEADDIM).reshape(1, -1),
            "m2_norm_w": m2_norm_w[i].reshape(1, -1),
            "w_m2_out": w_m2_out[i].astype(BF16),
            "w_o": w_o[i].astype(BF16),
            "norm2_w": norm2_w[i].reshape(1, -1),
            "w_up": w_up[i].astype(BF16),
            "w_down": w_down[i].astype(BF16),
        })
    return layers


def kernel(x_prompt, x_sample, norm1_w, w_in, lam_re, lam_im, log_dt, b_re, b_im, c_re, c_im, d_s5, w_glu, b_glu, w_s5_out, conv_w, conv_b, dt_bias, a_log, d_m2, m2_norm_w, w_m2_out, w_o, norm2_w, w_up, w_down, final_norm_w):
    layers = _prepare_layers(norm1_w, w_in, lam_re, lam_im, log_dt, b_re, b_im, c_re, c_im, d_s5,
                             w_glu, b_glu, w_s5_out, conv_w, conv_b, dt_bias, a_log, d_m2,
                             m2_norm_w, w_m2_out, w_o, norm2_w, w_up, w_down)
    fnw = final_norm_w.reshape(1, -1)
    return (_trunk(x_prompt, layers, fnw), _trunk(x_sample, layers, fnw))
```

```python
import functools
from typing import NamedTuple

import jax
import jax.numpy as jnp
from jax import lax
from jax.experimental import pallas as pl
from jax.experimental.pallas import tpu as pltpu

F32 = jnp.float32
BF16 = jnp.bfloat16

D_MODEL = 1024
DEPTH = 2
N_DIR = 2
EPS = 1e-6
S5_WIDTH = 768
S5_GROUP = 16
S5_GROUPS = 48
S5_STATE = 64
S5_LANES = S5_GROUPS * S5_STATE
M2_INNER = 1536
M2_HEADDIM = 64
M2_HEADS = 24
M2_GROUPS = 4
M2_STATE = 128
M2_CONV = 4
M2_GN = M2_GROUPS * M2_STATE
M2_CONV_DIM = M2_INNER + 2 * M2_GN
HEADS_PER_GROUP = M2_HEADS // M2_GROUPS
D_FF = 4096

LANES = 128
SUBLANES = 8
VMEM_LIMIT_BYTES = 56 * 1024 * 1024

LOG2E = 1.4426950408889634
DT_PAD = LANES
S5_SLAB_GROUPS = LANES // S5_GROUP
S5_SLABS = S5_WIDTH // LANES
S5_SLAB_STATES = S5_SLAB_GROUPS * S5_STATE

TM_INPROJ = 256
TM_MIX = 256
TM_MLP = 512
S5_BLOCK = 16
S5_ROWS = 128
SSD_CHUNK = 128
SSD_STEP_CHUNKS = 4
CONV_COLS = 256
CONV_SUB = 128
N_CHUNK_COLS = 512


def _cparams(flags=None):
    return pltpu.CompilerParams(dimension_semantics=None, vmem_limit_bytes=VMEM_LIMIT_BYTES, flags=flags)


class _LayerWeight(NamedTuple):
    stacked: jax.Array
    layer: int


def _resident(w):
    if isinstance(w, _LayerWeight):
        shape, layer = w.stacked.shape, w.layer
        return pl.BlockSpec((None,) + shape[1:], lambda *_: (layer,) + (0,) * (len(shape) - 1),
                            pipeline_mode=pl.Buffered(1))
    nd = w.ndim
    return pl.BlockSpec(w.shape, lambda *_: (0,) * nd, pipeline_mode=pl.Buffered(1))


def _operands(ws):
    return [w.stacked if isinstance(w, _LayerWeight) else w for w in ws]


def _silu(x):
    return x * (1.0 / (1.0 + jnp.exp(-x)))


def _sigmoid(x):
    return 1.0 / (1.0 + jnp.exp(-x))


def _softplus(x):
    return jnp.maximum(x, 0.0) + jnp.log1p(jnp.exp(-jnp.abs(x)))


def _rms(x, w):
    var = jnp.mean(x * x, axis=-1, keepdims=True)
    return x * lax.rsqrt(var + EPS) * w


def _dot(a, b):
    return jnp.dot(a, b, preferred_element_type=F32)


def _split3(x):
    hi = x.astype(BF16)
    r1 = x - hi.astype(F32)
    mid = r1.astype(BF16)
    lo = (r1 - mid.astype(F32)).astype(BF16)
    return hi, mid, lo


def _conv_silu_rows(xe, tm, first, last, w, b):
    before = jnp.where(first, 0.0, xe[tm:tm + SUBLANES])
    after = jnp.where(last, 0.0, xe[tm + SUBLANES:tm + 2 * SUBLANES])
    ext = jnp.concatenate([before, xe[:tm], after], axis=0)
    outs = []
    for r0 in range(0, tm, CONV_SUB):
        blk = ext[r0:r0 + CONV_SUB + 2 * SUBLANES]
        nr = blk.shape[0]
        mid = slice(SUBLANES, SUBLANES + CONV_SUB)
        acc = (w[0:1] * pltpu.roll(blk, 1, 0)[mid] + w[1:2] * blk[mid] + w[2:3] * pltpu.roll(blk, nr - 1, 0)[mid]
               + w[3:4] * pltpu.roll(blk, nr - 2, 0)[mid] + b)
        outs.append(_silu(acc))
    return jnp.concatenate(outs, axis=0)


def _inproj_kernel(x_ref, xp_ref, xn_ref, nw_ref, w_ref, dtb_ref, alog_ref, cw_ref, cb_ref,
                   u_ref, z_ref, xact_ref, dt_ref, g_ref, pcs_ref, *, tiles_per_seq):
    tm = x_ref.shape[0]
    i = pl.program_id(0)
    first = lax.rem(i, tiles_per_seq) == 0
    last = lax.rem(i + 1, tiles_per_seq) == 0
    he = _rms(jnp.concatenate([x_ref[...], xp_ref[...], xn_ref[...]], axis=0), nw_ref[...]).astype(BF16)
    h = he[:tm]
    ident = lambda v: v
    posts = (ident, ident, None, lambda v: _softplus(v + dtb_ref[...]), _sigmoid)
    off = 0
    for ref, post in zip((u_ref, z_ref, xact_ref, dt_ref, g_ref), posts):
        n = ref.shape[-1]
        step = CONV_COLS if post is None else N_CHUNK_COLS
        for c0 in range(0, n, step):
            c1 = min(c0 + step, n)
            wc = w_ref[:, off + c0:off + c1]
            if post is None:
                val = _conv_silu_rows(_dot(he, wc), tm, first, last, cw_ref[:, c0:c1], cb_ref[:, c0:c1])
            else:
                val = post(_dot(h, wc))
            ref[:, c0:c1] = val.astype(ref.dtype)
        off += n
    q = SSD_CHUNK
    r = lax.broadcasted_iota(jnp.int32, (q, q), 0)
    c = lax.broadcasted_iota(jnp.int32, (q, q), 1)
    low = jnp.where(c <= r, 1.0, 0.0).astype(BF16)
    upp = jnp.where(c >= r, 1.0, 0.0).astype(BF16)
    fwd_col = lax.broadcasted_iota(jnp.int32, (q, DT_PAD), 1) < M2_HEADS
    for r0 in range(0, dt_ref.shape[0], q):
        parts = _split3(dt_ref[r0:r0 + q, :] * (-jnp.exp(alog_ref[...])))
        cs_f = sum(_dot(low, p) for p in parts)
        cs_b = sum(_dot(upp, p) for p in parts)
        pcs_ref[r0:r0 + q, :] = jnp.where(fwd_col, cs_f, cs_b) * LOG2E


def _inproj(x2d, seq_len, norm_w, w_in_p, dtb_tl, alog_tl, conv_w, conv_b):
    t = x2d.shape[0]
    tm = TM_INPROJ
    assert tm % SSD_CHUNK == 0 and seq_len % tm == 0
    widths = (S5_WIDTH, M2_INNER, M2_CONV_DIM, DT_PAD, 2 * D_MODEL, DT_PAD)
    dtypes = (BF16, BF16, BF16, F32, BF16, F32)
    row = lambda n: pl.BlockSpec((tm, n), lambda i: (i, 0))
    n8 = tm // SUBLANES
    before = pl.BlockSpec((SUBLANES, D_MODEL), lambda i: (jnp.maximum(i * n8 - 1, 0), 0))
    after = pl.BlockSpec((SUBLANES, D_MODEL), lambda i: (jnp.minimum((i + 1) * n8, t // SUBLANES - 1), 0))
    consts = (norm_w, w_in_p, dtb_tl, alog_tl, conv_w, conv_b)
    return pl.pallas_call(
        functools.partial(_inproj_kernel, tiles_per_seq=seq_len // tm),
        grid=(t // tm,),
        in_specs=[row(D_MODEL), before, after] + [_resident(w) for w in consts],
        out_specs=[row(n) for n in widths],
        out_shape=[jax.ShapeDtypeStruct((t, n), dt) for n, dt in zip(widths, dtypes)],
        compiler_params=_cparams(),
        name="inproj",
    )(x2d, x2d, x2d, *_operands(consts))


def _s5_slab_halves():
    return [(k, h) for k in range(S5_SLABS) for h in range(2)]


S5_HALF = S5_BLOCK // 2
S5_GL = S5_BLOCK * S5_GROUP


def _s5_direction(ut_ref, we_ref, t0_ref, wy_ref, are_ref, aim_ref, ere_ref, eim_ref, car_ref, reverse, emit):
    nb = ut_ref.shape[1]
    ut = lambda lo, hi: ut_ref[0, :, lo:hi]

    @pl.when(pl.program_id(1) == 0)
    def _():
        car_ref[...] = jnp.zeros_like(car_ref)

    for gp in range(S5_GROUPS // 2):
        e = _dot(ut(gp * 2 * S5_GL, (gp + 1) * 2 * S5_GL), we_ref[gp])
        ere_ref[:, gp * LANES:(gp + 1) * LANES] = e[:, :LANES]
        eim_ref[:, gp * LANES:(gp + 1) * LANES] = e[:, LANES:]

    ar, ai = are_ref[...], aim_ref[...]

    def step(i, carry):
        sr, si = carry
        row = pl.ds(nb - 1 - i if reverse else i, 1)
        er, ei = ere_ref[row, :], eim_ref[row, :]
        ere_ref[row, :] = sr
        eim_ref[row, :] = si
        return ar * sr - ai * si + er, ar * si + ai * sr + ei

    sr, si = lax.fori_loop(0, nb, step, (car_ref[0:1, :], car_ref[1:2, :]), unroll=8)
    car_ref[0:1, :] = sr
    car_ref[1:2, :] = si

    for gp in range(S5_GROUPS // 2):
        sin = jnp.concatenate([ere_ref[:, gp * LANES:(gp + 1) * LANES], eim_ref[:, gp * LANES:(gp + 1) * LANES]],
                              axis=1).astype(BF16)
        yo = _dot(sin, wy_ref[gp])
        for gi in range(2):
            g = 2 * gp + gi
            emit(g, _dot(ut(g * S5_GL, (g + 1) * S5_GL), t0_ref[g]) + yo[:, gi * S5_GL:(gi + 1) * S5_GL])


def _s5_fwd_kernel(u_ref, p_ref, we_ref, t0_ref, wy_ref, are_ref, aim_ref, ut_ref, yt_ref,
                   stage_ref, ere_ref, eim_ref, car_ref):
    nb = u_ref.shape[1] // S5_BLOCK
    for k in range(S5_SLABS):
        stage_ref[k] = u_ref[0, :, k * LANES:(k + 1) * LANES].astype(F32)

    def offset_rows(k, r):
        return stage_ref[k, pl.ds(r, nb, stride=S5_BLOCK), :].astype(BF16)

    lhs = jnp.concatenate(
        [jnp.concatenate([offset_rows(k, S5_HALF * h + r) for r in range(S5_HALF)], axis=1)
         for k, h in _s5_slab_halves()], axis=0)
    perm = _dot(lhs, p_ref[...]).astype(BF16)
    for i, (k, h) in enumerate(_s5_slab_halves()):
        for gi in range(S5_SLAB_GROUPS):
            lo = (k * S5_SLAB_GROUPS + gi) * S5_GL + h * LANES
            ut_ref[0, :, lo:lo + LANES] = perm[i * nb:(i + 1) * nb, gi * LANES:(gi + 1) * LANES]

    def emit(g, yg):
        yt_ref[0, :, g * S5_GL:(g + 1) * S5_GL] = yg.astype(yt_ref.dtype)

    _s5_direction(ut_ref, we_ref, t0_ref, wy_ref, are_ref, aim_ref, ere_ref, eim_ref, car_ref, False, emit)


def _s5_bwd_kernel(ut_ref, ytf_ref, pt_ref, we_ref, t0_ref, wy_ref, are_ref, aim_ref, y_ref,
                   stage_ref, yt_ref, ere_ref, eim_ref, car_ref):
    nb = ut_ref.shape[1]

    def emit(g, yg):
        cols = slice(g * S5_GL, (g + 1) * S5_GL)
        yt_ref[:, cols] = (yg + ytf_ref[0, :, cols].astype(F32)).astype(yt_ref.dtype)

    _s5_direction(ut_ref, we_ref, t0_ref, wy_ref, are_ref, aim_ref, ere_ref, eim_ref, car_ref, True, emit)

    lhs = jnp.concatenate(
        [jnp.concatenate([yt_ref[:, (k * S5_SLAB_GROUPS + gi) * S5_GL + h * LANES:
                                 (k * S5_SLAB_GROUPS + gi) * S5_GL + (h + 1) * LANES]
                          for gi in range(S5_SLAB_GROUPS)], axis=1) for k, h in _s5_slab_halves()], axis=0)
    out = _dot(lhs, pt_ref[...])
    for i, (k, h) in enumerate(_s5_slab_halves()):
        for r in range(S5_HALF):
            stage_ref[k, pl.ds(S5_HALF * h + r, nb, stride=S5_BLOCK), :] = out[i * nb:(i + 1) * nb, r * LANES:(r + 1) * LANES]
    for k in range(S5_SLABS):
        y_ref[0, :, k * LANES:(k + 1) * LANES] = stage_ref[k].astype(y_ref.dtype)


def _s5_mixer_scan(u, sp, perm, perm_t):
    b, l, _ = u.shape
    nb = min(S5_ROWS, l // S5_BLOCK)
    nt = nb * S5_BLOCK
    nc = l // nt
    width = S5_BLOCK * S5_WIDTH
    tok_f = pl.BlockSpec((1, nt, S5_WIDTH), lambda i, c: (i, c, 0))
    blk_f = pl.BlockSpec((1, nb, width), lambda i, c: (i, c, 0))
    tok_b = pl.BlockSpec((1, nt, S5_WIDTH), lambda i, c: (i, nc - 1 - c, 0))
    blk_b = pl.BlockSpec((1, nb, width), lambda i, c: (i, nc - 1 - c, 0))
    blocked = jax.ShapeDtypeStruct((b, l // S5_BLOCK, width), BF16)
    state = [pltpu.VMEM((nb, S5_LANES), F32), pltpu.VMEM((nb, S5_LANES), F32), pltpu.VMEM((SUBLANES, S5_LANES), F32)]
    stage = pltpu.VMEM((S5_SLABS, nt, LANES), F32)
    f, r = sp
    wf = (perm, f["we"], f["t0"], f["wy"], f["are"], f["aim"])
    ut, ytf = pl.pallas_call(
        _s5_fwd_kernel,
        grid=(b, nc),
        in_specs=[tok_f] + [_resident(w) for w in wf],
        out_specs=[blk_f, blk_f],
        out_shape=[blocked, blocked],
        scratch_shapes=[stage] + state,
        compiler_params=_cparams(),
        name="s5fwd",
    )(u, *_operands(wf))
    wr = (perm_t, r["we"], r["t0"], r["wy"], r["are"], r["aim"])
    return pl.pallas_call(
        _s5_bwd_kernel,
        grid=(b, nc),
        in_specs=[blk_b, blk_b] + [_resident(w) for w in wr],
        out_specs=tok_b,
        out_shape=jax.ShapeDtypeStruct(u.shape, BF16),
        scratch_shapes=[stage, pltpu.VMEM((nb, width), BF16)] + state,
        compiler_params=_cparams(),
        name="s5bwd",
    )(ut, ytf, *_operands(wr))


def _ssd_chunk(xa, pcs_tl, pcs_hl2, dt_hl2, st_ref, d, reverse):
    q = xa.shape[0]
    hs = slice(d * M2_HEADS, (d + 1) * M2_HEADS)
    dt_hl = dt_hl2[hs, :]
    pcs_hl = pcs_hl2[hs, :]
    r = lax.broadcasted_iota(jnp.int32, (q, q), 0)
    c = lax.broadcasted_iota(jnp.int32, (q, q), 1)
    end = 0 if reverse else q - 1
    tot_hl = pcs_hl[:, end:end + 1]
    keep = (c >= r) if reverse else (c <= r)
    rowp_hl = pcs_hl - jnp.log2(dt_hl)
    wrow_hl = jnp.exp2(tot_hl - rowp_hl)
    dec_hl = jnp.exp2(tot_hl)

    lane = lax.broadcasted_iota(jnp.int32, (q, LANES), 1)
    first_head = lane < M2_HEADDIM
    first_head_row = lax.broadcasted_iota(jnp.int32, (1, LANES), 1) < M2_HEADDIM

    ys = []
    for g in range(M2_GROUPS):
        bm = xa[:, M2_INNER + g * M2_STATE:M2_INNER + (g + 1) * M2_STATE]
        cm = xa[:, M2_INNER + M2_GN + g * M2_STATE:M2_INNER + M2_GN + (g + 1) * M2_STATE]
        cb = lax.dot_general(cm, bm, (((1,), (1,)), ((), ())), preferred_element_type=F32)
        bt = bm.astype(F32).T
        st_g = st_ref[d, g]
        yoff = _dot(cm, st_g.astype(BF16))
        new_cols = []
        for jp in range(HEADS_PER_GROUP // 2):
            h0 = g * HEADS_PER_GROUP + 2 * jp
            pair = slice(h0 * M2_HEADDIM, (h0 + 2) * M2_HEADDIM)
            xp = xa[:, pair]
            xpf = xp.astype(F32)
            xbd = jnp.concatenate([jnp.where(first_head, xpf, 0.0), jnp.where(first_head, 0.0, xpf)],
                                  axis=0).astype(BF16)
            ms, ws, cols = [], [], []
            for h in (h0, h0 + 1):
                col = jnp.broadcast_to(pcs_tl[:, d * M2_HEADS + h:d * M2_HEADS + h + 1], (q, LANES))
                decay_dt = jnp.where(keep, jnp.exp2(jnp.tile(col, (1, q // LANES)) - rowp_hl[h:h + 1, :]), 0.0)
                ms.append((cb * decay_dt).astype(BF16))
                ws.append((bt * wrow_hl[h:h + 1, :]).astype(BF16))
                cols.append(col)
            m2 = jnp.concatenate(ms, axis=1)
            w2 = jnp.concatenate(ws, axis=1)
            yslab = yoff[:, 2 * jp * M2_HEADDIM:(2 * jp + 2) * M2_HEADDIM]
            ys.append(_dot(m2, xbd) + jnp.exp2(jnp.where(first_head, cols[0], cols[1])) * yslab)
            sslab = st_g[:, 2 * jp * M2_HEADDIM:(2 * jp + 2) * M2_HEADDIM]
            dec_row = jnp.where(first_head_row, dec_hl[h0:h0 + 1, :], dec_hl[h0 + 1:h0 + 2, :])
            new_cols.append(dec_row * sslab + _dot(w2, xbd))
        st_ref[d, g] = jnp.concatenate(new_cols, axis=1)
    return jnp.concatenate(ys, axis=1)


def _ssd_kernel(xf_ref, xb_ref, pf_ref, pb_ref, phf_ref, phb_ref, dthf_ref, dthb_ref, yf_ref, yb_ref, st_ref):
    @pl.when(pl.program_id(1) == 0)
    def _():
        st_ref[...] = jnp.zeros_like(st_ref)

    q = SSD_CHUNK
    n_sub = xf_ref.shape[1] // q
    for j in range(n_sub):
        rf = slice(j * q, (j + 1) * q)
        rb = slice((n_sub - 1 - j) * q, (n_sub - j) * q)
        yf_ref[0, rf, :] = _ssd_chunk(xf_ref[0, rf, :], pf_ref[0, rf, :], phf_ref[0, :, rf], dthf_ref[0, :, rf],
                                      st_ref, 0, False).astype(yf_ref.dtype)
        yb_ref[0, rb, :] = _ssd_chunk(xb_ref[0, rb, :], pb_ref[0, rb, :], phb_ref[0, :, rb], dthb_ref[0, :, rb],
                                      st_ref, 1, True).astype(yb_ref.dtype)


def _ssd(xact, pcs_tl, pcs_hl, dt_hl):
    b, l, _ = xact.shape
    q = SSD_CHUNK * SSD_STEP_CHUNKS
    nc = l // q
    fwd3 = lambda n: pl.BlockSpec((1, q, n), lambda i, c: (i, c, 0))
    bwd3 = lambda n: pl.BlockSpec((1, q, n), lambda i, c: (i, nc - 1 - c, 0))
    hl_f = pl.BlockSpec((1, N_DIR * M2_HEADS, q), lambda i, c: (i, 0, c))
    hl_b = pl.BlockSpec((1, N_DIR * M2_HEADS, q), lambda i, c: (i, 0, nc - 1 - c))
    return pl.pallas_call(
        _ssd_kernel,
        grid=(b, nc),
        in_specs=[fwd3(M2_CONV_DIM), bwd3(M2_CONV_DIM), fwd3(DT_PAD), bwd3(DT_PAD), hl_f, hl_b, hl_f, hl_b],
        out_specs=[fwd3(M2_INNER), bwd3(M2_INNER)],
        out_shape=[jax.ShapeDtypeStruct((b, l, M2_INNER), BF16)] * 2,
        scratch_shapes=[pltpu.VMEM((N_DIR, M2_GROUPS, M2_STATE, HEADS_PER_GROUP * M2_HEADDIM), F32)],
        compiler_params=_cparams(),
        name="ssd",
    )(xact, xact, pcs_tl, pcs_tl, pcs_hl, pcs_hl, dt_hl, dt_hl)


def _mix_kernel(x_ref, g_ref, u_ref, s5y_ref, xs_ref, mf_ref, mb_ref, z_ref,
                ds5_ref, wglu_ref, bglu_ref, ws5_ref, dm2_ref, nm2_ref, wm2_ref, wo_ref, o_ref):
    f = lambda ref: ref[...].astype(F32)
    y = ds5_ref[...] * f(u_ref) + f(s5y_ref)
    h = jax.nn.gelu(y)
    h = h * _sigmoid(_dot(h.astype(BF16), wglu_ref[...]) + bglu_ref[...])
    s5 = _dot(h.astype(BF16), ws5_ref[...])
    ym = dm2_ref[...] * f(xs_ref) + f(mf_ref) + f(mb_ref)
    gz = ym * _silu(f(z_ref))
    gw = M2_INNER // M2_GROUPS
    parts = []
    for i in range(M2_GROUPS):
        gs = gz[:, i * gw:(i + 1) * gw]
        parts.append(gs * lax.rsqrt(jnp.mean(gs * gs, axis=-1, keepdims=True) + EPS))
    gn = jnp.concatenate(parts, axis=1) * nm2_ref[...]
    m2 = _dot(gn.astype(BF16), wm2_ref[...])
    gates = f(g_ref)
    merged = gates[:, :D_MODEL] * s5 + gates[:, D_MODEL:] * m2
    o_ref[...] = x_ref[...] + _dot(merged.astype(BF16), wo_ref[...])


def _mix(x2d, gates, u, s5y, xact, mf, mb, z, mp):
    t = x2d.shape[0]
    tm = TM_MIX
    row = lambda n: pl.BlockSpec((tm, n), lambda i: (i, 0))
    weights = (mp["d_s5"], mp["w_glu"], mp["b_glu"], mp["w_s5_out"], mp["d_m2"], mp["m2_norm_w"],
               mp["w_m2_out"], mp["w_o"])
    return pl.pallas_call(
        _mix_kernel,
        grid=(t // tm,),
        in_specs=[row(D_MODEL), row(2 * D_MODEL), row(S5_WIDTH), row(S5_WIDTH),
                  row(M2_INNER), row(M2_INNER), row(M2_INNER), row(M2_INNER)]
        + [_resident(w) for w in weights],
        out_specs=row(D_MODEL),
        out_shape=jax.ShapeDtypeStruct((t, D_MODEL), F32),
        compiler_params=_cparams(),
        name="mix",
    )(x2d, gates, u, s5y, xact, mf, mb, z, *_operands(weights))


def _mlp_kernel(x_ref, n2_ref, wup_ref, wdn_ref, fn_ref, o_ref, *, final):
    x = x_ref[...]
    h = _rms(x, n2_ref[...]).astype(BF16)
    acc = x
    for c0 in range(0, D_FF, 2 * N_CHUNK_COLS):
        c1 = c0 + 2 * N_CHUNK_COLS
        hid = jnp.maximum(_dot(h, wup_ref[:, c0:c1]), 0.0)
        acc = acc + _dot((hid * hid).astype(BF16), wdn_ref[c0:c1, :])
    o_ref[...] = _rms(acc, fn_ref[...]) if final else acc


def _mlp(x2d, norm2_w, w_up, w_down, final_norm_w, final):
    t = x2d.shape[0]
    tm = TM_MLP
    row = pl.BlockSpec((tm, D_MODEL), lambda i: (i, 0))
    weights = (norm2_w, w_up, w_down, final_norm_w)
    return pl.pallas_call(
        functools.partial(_mlp_kernel, final=final),
        grid=(t // tm,),
        in_specs=[row] + [_resident(w) for w in weights],
        out_specs=row,
        out_shape=jax.ShapeDtypeStruct((t, D_MODEL), F32),
        compiler_params=_cparams(),
        name="mlp",
    )(x2d, *_operands(weights))


def _s5_perm():
    n = S5_SLAB_GROUPS * LANES
    i = jnp.arange(n)
    r_lo, gi, c = i // LANES, (i % LANES) // S5_GROUP, i % S5_GROUP
    dst = gi * LANES + r_lo * S5_GROUP + c
    return (dst[:, None] == jnp.arange(n)[None, :]).astype(BF16)


def _pair_rows_cols(m_re, m_im, rows_are_states):
    g, a, b = m_re.shape
    z = jnp.zeros_like(m_re[0::2])
    if rows_are_states:
        top = jnp.concatenate([m_re[0::2], z], axis=2), jnp.concatenate([z, m_re[1::2]], axis=2)
        bot = jnp.concatenate([m_im[0::2], z], axis=2), jnp.concatenate([z, m_im[1::2]], axis=2)
        return jnp.concatenate([top[0], top[1], bot[0], bot[1]], axis=1)
    left = jnp.concatenate([m_re[0::2], z, m_im[0::2], z], axis=2)
    right = jnp.concatenate([z, m_re[1::2], z, m_im[1::2]], axis=2)
    return jnp.concatenate([left, right], axis=1)


def _s5_params(lam_re, lam_im, log_dt, b_re, b_im, c_re, c_im):
    rb = S5_BLOCK
    delta = jnp.exp(log_dt)[..., None]
    mag = jnp.exp(lam_re * delta)
    a_re = mag * jnp.cos(lam_im * delta)
    a_im = mag * jnp.sin(lam_im * delta)
    inv = 1.0 / (lam_re * lam_re + lam_im * lam_im)
    q_re = ((a_re - 1.0) * lam_re + a_im * lam_im) * inv
    q_im = (a_im * lam_re - (a_re - 1.0) * lam_im) * inv
    bbar_re = q_re[..., None] * b_re - q_im[..., None] * b_im
    bbar_im = q_re[..., None] * b_im + q_im[..., None] * b_re
    kk = jnp.arange(rb + 1, dtype=F32).reshape(-1, 1, 1, 1)
    pk_mag = jnp.exp(kk * (lam_re * delta))
    pk_re = pk_mag * jnp.cos(kk * (lam_im * delta))
    pk_im = pk_mag * jnp.sin(kk * (lam_im * delta))
    r = jnp.arange(rb)
    out = []
    for d in range(N_DIR):
        fwd = d == 0
        pr, pi = pk_re[:, d], pk_im[:, d]
        w_re = pr[..., None] * bbar_re[d] - pi[..., None] * bbar_im[d]
        w_im = pr[..., None] * bbar_im[d] + pi[..., None] * bbar_re[d]
        k3 = kk[:rb, 0]
        ni_mag = jnp.exp(-k3 * (lam_re[d] * delta[d]))
        ni_re = ni_mag * jnp.cos(k3 * (lam_im[d] * delta[d]))
        ni_im = -ni_mag * jnp.sin(k3 * (lam_im[d] * delta[d]))
        src_re = ni_re[..., None] * bbar_re[d] - ni_im[..., None] * bbar_im[d]
        src_im = ni_re[..., None] * bbar_im[d] + ni_im[..., None] * bbar_re[d]
        dst_re = c_re[d][None] * pr[:rb, :, None, :] - c_im[d][None] * pi[:rb, :, None, :]
        dst_im = c_re[d][None] * pi[:rb, :, None, :] + c_im[d][None] * pr[:rb, :, None, :]
        if not fwd:
            src_re, src_im = w_re[:rb], w_im[:rb]
            dst_re = c_re[d][None] * ni_re[:, :, None, :] - c_im[d][None] * ni_im[:, :, None, :]
            dst_im = c_re[d][None] * ni_im[:, :, None, :] + c_im[d][None] * ni_re[:, :, None, :]
        n = rb * S5_GROUP
        src = jnp.transpose(jnp.concatenate([src_re, src_im], axis=2), (1, 0, 3, 2)).reshape(S5_GROUPS, n, -1)
        dst = jnp.transpose(jnp.concatenate([dst_re, -dst_im], axis=3), (1, 3, 0, 2)).reshape(S5_GROUPS, -1, n)
        t0 = jnp.matmul(src, dst, precision=lax.Precision.HIGHEST)
        causal = (r[None, :] >= r[:, None]) if fwd else (r[None, :] <= r[:, None])
        t0 = jnp.where(jnp.repeat(jnp.repeat(causal, S5_GROUP, axis=0), S5_GROUP, axis=1)[None], t0, 0.0)
        ke = (rb - 1 - r) if fwd else r
        we_re = jnp.transpose(w_re[ke], (1, 0, 3, 2)).reshape(S5_GROUPS, rb * S5_GROUP, S5_STATE)
        we_im = jnp.transpose(w_im[ke], (1, 0, 3, 2)).reshape(S5_GROUPS, rb * S5_GROUP, S5_STATE)
        ky = (r + 1) if fwd else (rb - r)
        g_re = c_re[d][None] * pr[ky][:, :, None, :] - c_im[d][None] * pi[ky][:, :, None, :]
        g_im = c_re[d][None] * pi[ky][:, :, None, :] + c_im[d][None] * pr[ky][:, :, None, :]
        wy_re = jnp.transpose(g_re, (1, 3, 0, 2)).reshape(S5_GROUPS, S5_STATE, rb * S5_GROUP)
        wy_im = jnp.transpose(-g_im, (1, 3, 0, 2)).reshape(S5_GROUPS, S5_STATE, rb * S5_GROUP)
        out.append({
            "we": _pair_rows_cols(we_re, we_im, False).astype(BF16),
            "t0": t0.astype(BF16),
            "wy": _pair_rows_cols(wy_re, wy_im, True).astype(BF16),
            "are": pr[rb].reshape(1, S5_LANES), "aim": pi[rb].reshape(1, S5_LANES),
        })
    return out


def _m2_params(dt_bias, a_log):
    pad = DT_PAD - N_DIR * M2_HEADS
    flat = lambda v: v.reshape(1, N_DIR * M2_HEADS)
    return {
        "dtb_tl": jnp.pad(flat(dt_bias), ((0, 0), (0, pad))),
        "alog_tl": jnp.pad(flat(a_log), ((0, 0), (0, pad))),
    }


def _pad_w_in(w_in):
    o_z = S5_WIDTH
    o_xbc = o_z + M2_INNER
    o_dt = o_xbc + M2_CONV_DIM
    o_g = o_dt + N_DIR * M2_HEADS
    dt = jnp.pad(w_in[:, o_dt:o_g], ((0, 0), (0, DT_PAD - N_DIR * M2_HEADS)))
    return jnp.concatenate([w_in[:, :o_dt], dt, w_in[:, o_g:]], axis=1).astype(BF16)


def _layer(x2d, b, l, lp, final_norm_w, final):
    u, z, xact, dt, gates, pcs = _inproj(x2d, l, lp["norm1_w"], lp["w_in"], lp["m2"]["dtb_tl"],
                                         lp["m2"]["alog_tl"], lp["conv_w"], lp["conv_b"])
    s5y = _s5_mixer_scan(u.reshape(b, l, S5_WIDTH), lp["s5"], lp["perm"], lp["perm_t"])
    xact = xact.reshape(b, l, M2_CONV_DIM)
    heads_on_sublanes = lambda v: jnp.swapaxes(v.reshape(b, l, DT_PAD)[:, :, :N_DIR * M2_HEADS], 1, 2)
    mf, mb = _ssd(xact, pcs.reshape(b, l, DT_PAD), heads_on_sublanes(pcs), heads_on_sublanes(dt))
    t = b * l
    x1 = _mix(x2d, gates, u, s5y.reshape(t, -1), xact.reshape(t, -1), mf.reshape(t, -1), mb.reshape(t, -1), z, lp)
    return _mlp(x1, lp["norm2_w"], lp["w_up"], lp["w_down"], final_norm_w, final)


def _trunk(x, layers, final_norm_w):
    b, l, _ = x.shape
    x2d = x.reshape(b * l, D_MODEL)
    for i, lp in enumerate(layers):
        x2d = _layer(x2d, b, l, lp, final_norm_w, i == len(layers) - 1)
    return x2d.reshape(b, l, D_MODEL)


def _prepare_layers(norm1_w, w_in, lam_re, lam_im, log_dt, b_re, b_im, c_re, c_im, d_s5, w_glu,
                    b_glu, w_s5_out, conv_w, conv_b, dt_bias, a_log, d_m2, m2_norm_w, w_m2_out,
                    w_o, norm2_w, w_up, w_down):
    depth = norm1_w.shape[0]
    row = lambda v: v.reshape(depth, 1, -1)
    stacked = {
        "norm1_w": row(norm1_w),
        "w_in": jax.vmap(_pad_w_in)(w_in),
        "s5": jax.vmap(_s5_params)(lam_re, lam_im, log_dt, b_re, b_im, c_re, c_im),
        "d_s5": row(d_s5),
        "w_glu": w_glu.astype(BF16),
        "b_glu": row(b_glu),
        "w_s5_out": w_s5_out.astype(BF16),
        "conv_w": conv_w,
        "conv_b": row(conv_b),
        "m2": jax.vmap(_m2_params)(dt_bias, a_log),
        "d_m2": row(jnp.repeat(d_m2, M2_HEADDIM, axis=1)),
        "m2_norm_w": row(m2_norm_w),
        "w_m2_out": w_m2_out.astype(BF16),
        "w_o": w_o.astype(BF16),
        "norm2_w": row(norm2_w),
        "w_up": w_up.astype(BF16),
        "w_down": w_down.astype(BF16),
    }
    perm = _s5_perm()
    layers = []
    for i in range(depth):
        lp = jax.tree.map(lambda v: _LayerWeight(v, i), stacked)
        lp["perm"], lp["perm_t"] = perm, perm.T
        layers.append(lp)
    return layers


def kernel(x_prompt, x_sample, norm1_w, w_in, lam_re, lam_im, log_dt, b_re, b_im, c_re, c_im, d_s5, w_glu, b_glu, w_s5_out, conv_w, conv_b, dt_bias, a_log, d_m2, m2_norm_w, w_m2_out, w_o, norm2_w, w_up, w_down, final_norm_w):
    layers = _prepare_layers(norm1_w, w_in, lam_re, lam_im, log_dt, b_re, b_im, c_re, c_im, d_s5,
                             w_glu, b_glu, w_s5_out, conv_w, conv_b, dt_bias, a_log, d_m2,
                             m2_norm_w, w_m2_out, w_o, norm2_w, w_up, w_down)
    fnw = final_norm_w.reshape(1, -1)
    return (_trunk(x_prompt, layers, fnw), _trunk(x_sample, layers, fnw))
```

```python
import functools
from typing import NamedTuple

import jax
import jax.numpy as jnp
from jax import lax
from jax.experimental import pallas as pl
from jax.experimental.pallas import tpu as pltpu

F32 = jnp.float32
BF16 = jnp.bfloat16

D_MODEL = 1024
DEPTH = 2
N_DIR = 2
EPS = 1e-6
S5_WIDTH = 768
S5_GROUP = 16
S5_GROUPS = 48
S5_STATE = 64
S5_LANES = S5_GROUPS * S5_STATE
M2_INNER = 1536
M2_HEADDIM = 64
M2_HEADS = 24
M2_GROUPS = 4
M2_STATE = 128
M2_CONV = 4
M2_GN = M2_GROUPS * M2_STATE
M2_CONV_DIM = M2_INNER + 2 * M2_GN
HEADS_PER_GROUP = M2_HEADS // M2_GROUPS
D_FF = 4096

LANES = 128
SUBLANES = 8
VMEM_LIMIT_BYTES = 56 * 1024 * 1024

LOG2E = 1.4426950408889634
DT_PAD = LANES
S5_SLAB_GROUPS = LANES // S5_GROUP
S5_SLABS = S5_WIDTH // LANES
S5_SLAB_STATES = S5_SLAB_GROUPS * S5_STATE

TM_INPROJ = 256
TM_MIX = 256
TM_MLP = 512
S5_BLOCK = 16
S5_ROWS = 128
SSD_CHUNK = 128
SSD_STEP_CHUNKS = 4
CONV_COLS = 256
CONV_SUB = 128
N_CHUNK_COLS = 512


def _cparams(flags=None):
    return pltpu.CompilerParams(dimension_semantics=None, vmem_limit_bytes=VMEM_LIMIT_BYTES, flags=flags)


class _LayerWeight(NamedTuple):
    stacked: jax.Array
    layer: int


def _resident(w):
    if isinstance(w, _LayerWeight):
        shape, layer = w.stacked.shape, w.layer
        return pl.BlockSpec((None,) + shape[1:], lambda *_: (layer,) + (0,) * (len(shape) - 1),
                            pipeline_mode=pl.Buffered(1))
    nd = w.ndim
    return pl.BlockSpec(w.shape, lambda *_: (0,) * nd, pipeline_mode=pl.Buffered(1))


def _operands(ws):
    return [w.stacked if isinstance(w, _LayerWeight) else w for w in ws]


def _silu(x):
    hx = 0.5 * x
    return hx * jnp.tanh(hx) + hx


def _sigmoid(x):
    return 0.5 * jnp.tanh(0.5 * x) + 0.5


def _softplus(x):
    return jnp.maximum(x, 0.0) + jnp.log1p(jnp.exp(-jnp.abs(x)))


def _rms(x, w):
    var = jnp.mean(x * x, axis=-1, keepdims=True)
    return x * lax.rsqrt(var + EPS) * w


def _dot(a, b):
    return jnp.dot(a, b, preferred_element_type=F32)


def _split3(x):
    hi = x.astype(BF16)
    r1 = x - hi.astype(F32)
    mid = r1.astype(BF16)
    lo = (r1 - mid.astype(F32)).astype(BF16)
    return hi, mid, lo


def _conv_silu_rows(xe, tm, first, last, w, b):
    before = jnp.where(first, 0.0, xe[tm:tm + SUBLANES])
    after = jnp.where(last, 0.0, xe[tm + SUBLANES:tm + 2 * SUBLANES])
    ext = jnp.concatenate([before, xe[:tm], after], axis=0)
    outs = []
    for r0 in range(0, tm, CONV_SUB):
        blk = ext[r0:r0 + CONV_SUB + 2 * SUBLANES]
        nr = blk.shape[0]
        mid = slice(SUBLANES, SUBLANES + CONV_SUB)
        acc = (w[0:1] * pltpu.roll(blk, 1, 0)[mid] + w[1:2] * blk[mid] + w[2:3] * pltpu.roll(blk, nr - 1, 0)[mid]
               + w[3:4] * pltpu.roll(blk, nr - 2, 0)[mid] + b)
        outs.append(_silu(acc))
    return jnp.concatenate(outs, axis=0)


def _inproj_kernel(x_ref, xp_ref, xn_ref, nw_ref, w_ref, dtb_ref, alog_ref, cw_ref, cb_ref,
                   u_ref, z_ref, xact_ref, dt_ref, g_ref, pcs_ref, *, tiles_per_seq):
    tm = x_ref.shape[0]
    i = pl.program_id(0)
    first = lax.rem(i, tiles_per_seq) == 0
    last = lax.rem(i + 1, tiles_per_seq) == 0
    he = _rms(jnp.concatenate([x_ref[...], xp_ref[...], xn_ref[...]], axis=0), nw_ref[...]).astype(BF16)
    h = he[:tm]
    ident = lambda v: v
    posts = (ident, ident, None, lambda v: _softplus(v + dtb_ref[...]), _sigmoid)
    off = 0
    for ref, post in zip((u_ref, z_ref, xact_ref, dt_ref, g_ref), posts):
        n = ref.shape[-1]
        step = CONV_COLS if post is None else N_CHUNK_COLS
        for c0 in range(0, n, step):
            c1 = min(c0 + step, n)
            wc = w_ref[:, off + c0:off + c1]
            if post is None:
                val = _conv_silu_rows(_dot(he, wc), tm, first, last, cw_ref[:, c0:c1], cb_ref[:, c0:c1])
            else:
                val = post(_dot(h, wc))
            ref[:, c0:c1] = val.astype(ref.dtype)
        off += n
    q = SSD_CHUNK
    r = lax.broadcasted_iota(jnp.int32, (q, q), 0)
    c = lax.broadcasted_iota(jnp.int32, (q, q), 1)
    low = jnp.where(c <= r, 1.0, 0.0).astype(BF16)
    upp = jnp.where(c >= r, 1.0, 0.0).astype(BF16)
    fwd_col = lax.broadcasted_iota(jnp.int32, (q, DT_PAD), 1) < M2_HEADS
    for r0 in range(0, dt_ref.shape[0], q):
        parts = _split3(dt_ref[r0:r0 + q, :] * (-jnp.exp(alog_ref[...])))
        cs_f = sum(_dot(low, p) for p in parts)
        cs_b = sum(_dot(upp, p) for p in parts)
        pcs_ref[r0:r0 + q, :] = jnp.where(fwd_col, cs_f, cs_b) * LOG2E


def _inproj(x2d, seq_len, norm_w, w_in_p, dtb_tl, alog_tl, conv_w, conv_b):
    t = x2d.shape[0]
    tm = TM_INPROJ
    assert tm % SSD_CHUNK == 0 and seq_len % tm == 0
    widths = (S5_WIDTH, M2_INNER, M2_CONV_DIM, DT_PAD, 2 * D_MODEL, DT_PAD)
    dtypes = (BF16, BF16, BF16, F32, BF16, F32)
    row = lambda n: pl.BlockSpec((tm, n), lambda i: (i, 0))
    n8 = tm // SUBLANES
    before = pl.BlockSpec((SUBLANES, D_MODEL), lambda i: (jnp.maximum(i * n8 - 1, 0), 0))
    after = pl.BlockSpec((SUBLANES, D_MODEL), lambda i: (jnp.minimum((i + 1) * n8, t // SUBLANES - 1), 0))
    consts = (norm_w, w_in_p, dtb_tl, alog_tl, conv_w, conv_b)
    return pl.pallas_call(
        functools.partial(_inproj_kernel, tiles_per_seq=seq_len // tm),
        grid=(t // tm,),
        in_specs=[row(D_MODEL), before, after] + [_resident(w) for w in consts],
        out_specs=[row(n) for n in widths],
        out_shape=[jax.ShapeDtypeStruct((t, n), dt) for n, dt in zip(widths, dtypes)],
        compiler_params=_cparams(),
        name="inproj",
    )(x2d, x2d, x2d, *_operands(consts))


def _s5_slab_halves():
    return [(k, h) for k in range(S5_SLABS) for h in range(2)]


S5_HALF = S5_BLOCK // 2
S5_GL = S5_BLOCK * S5_GROUP


def _s5_direction(ut_ref, we_ref, t0_ref, wy_ref, are_ref, aim_ref, ere_ref, eim_ref, car_ref, reverse, emit):
    nb = ut_ref.shape[1]
    ut = lambda lo, hi: ut_ref[0, :, lo:hi]

    @pl.when(pl.program_id(1) == 0)
    def _():
        car_ref[...] = jnp.zeros_like(car_ref)

    for gp in range(S5_GROUPS // 2):
        e = _dot(ut(gp * 2 * S5_GL, (gp + 1) * 2 * S5_GL), we_ref[gp])
        ere_ref[:, gp * LANES:(gp + 1) * LANES] = e[:, :LANES]
        eim_ref[:, gp * LANES:(gp + 1) * LANES] = e[:, LANES:]

    ar, ai = are_ref[...], aim_ref[...]

    def step(i, carry):
        sr, si = carry
        row = pl.ds(nb - 1 - i if reverse else i, 1)
        er, ei = ere_ref[row, :], eim_ref[row, :]
        ere_ref[row, :] = sr
        eim_ref[row, :] = si
        return ar * sr - ai * si + er, ar * si + ai * sr + ei

    sr, si = lax.fori_loop(0, nb, step, (car_ref[0:1, :], car_ref[1:2, :]), unroll=8)
    car_ref[0:1, :] = sr
    car_ref[1:2, :] = si

    for gp in range(S5_GROUPS // 2):
        sin = jnp.concatenate([ere_ref[:, gp * LANES:(gp + 1) * LANES], eim_ref[:, gp * LANES:(gp + 1) * LANES]],
                              axis=1).astype(BF16)
        yo = _dot(sin, wy_ref[gp])
        for gi in range(2):
            g = 2 * gp + gi
            emit(g, _dot(ut(g * S5_GL, (g + 1) * S5_GL), t0_ref[g]) + yo[:, gi * S5_GL:(gi + 1) * S5_GL])


def _s5_fwd_kernel(u_ref, p_ref, we_ref, t0_ref, wy_ref, are_ref, aim_ref, ut_ref, yt_ref,
                   stage_ref, ere_ref, eim_ref, car_ref):
    nb = u_ref.shape[1] // S5_BLOCK
    for k in range(S5_SLABS):
        stage_ref[k] = u_ref[0, :, k * LANES:(k + 1) * LANES].astype(F32)

    def offset_rows(k, r):
        return stage_ref[k, pl.ds(r, nb, stride=S5_BLOCK), :].astype(BF16)

    lhs = jnp.concatenate(
        [jnp.concatenate([offset_rows(k, S5_HALF * h + r) for r in range(S5_HALF)], axis=1)
         for k, h in _s5_slab_halves()], axis=0)
    perm = _dot(lhs, p_ref[...]).astype(BF16)
    for i, (k, h) in enumerate(_s5_slab_halves()):
        for gi in range(S5_SLAB_GROUPS):
            lo = (k * S5_SLAB_GROUPS + gi) * S5_GL + h * LANES
            ut_ref[0, :, lo:lo + LANES] = perm[i * nb:(i + 1) * nb, gi * LANES:(gi + 1) * LANES]

    def emit(g, yg):
        yt_ref[0, :, g * S5_GL:(g + 1) * S5_GL] = yg.astype(yt_ref.dtype)

    _s5_direction(ut_ref, we_ref, t0_ref, wy_ref, are_ref, aim_ref, ere_ref, eim_ref, car_ref, False, emit)


def _s5_bwd_kernel(ut_ref, ytf_ref, pt_ref, we_ref, t0_ref, wy_ref, are_ref, aim_ref, y_ref,
                   stage_ref, yt_ref, ere_ref, eim_ref, car_ref):
    nb = ut_ref.shape[1]

    def emit(g, yg):
        cols = slice(g * S5_GL, (g + 1) * S5_GL)
        yt_ref[:, cols] = (yg + ytf_ref[0, :, cols].astype(F32)).astype(yt_ref.dtype)

    _s5_direction(ut_ref, we_ref, t0_ref, wy_ref, are_ref, aim_ref, ere_ref, eim_ref, car_ref, True, emit)

    lhs = jnp.concatenate(
        [jnp.concatenate([yt_ref[:, (k * S5_SLAB_GROUPS + gi) * S5_GL + h * LANES:
                                 (k * S5_SLAB_GROUPS + gi) * S5_GL + (h + 1) * LANES]
                          for gi in range(S5_SLAB_GROUPS)], axis=1) for k, h in _s5_slab_halves()], axis=0)
    out = _dot(lhs, pt_ref[...])
    for i, (k, h) in enumerate(_s5_slab_halves()):
        for r in range(S5_HALF):
            stage_ref[k, pl.ds(S5_HALF * h + r, nb, stride=S5_BLOCK), :] = out[i * nb:(i + 1) * nb, r * LANES:(r + 1) * LANES]
    for k in range(S5_SLABS):
        y_ref[0, :, k * LANES:(k + 1) * LANES] = stage_ref[k].astype(y_ref.dtype)


def _s5_mixer_scan(u, sp, perm, perm_t):
    b, l, _ = u.shape
    nb = min(S5_ROWS, l // S5_BLOCK)
    nt = nb * S5_BLOCK
    nc = l // nt
    width = S5_BLOCK * S5_WIDTH
    tok_f = pl.BlockSpec((1, nt, S5_WIDTH), lambda i, c: (i, c, 0))
    blk_f = pl.BlockSpec((1, nb, width), lambda i, c: (i, c, 0))
    tok_b = pl.BlockSpec((1, nt, S5_WIDTH), lambda i, c: (i, nc - 1 - c, 0))
    blk_b = pl.BlockSpec((1, nb, width), lambda i, c: (i, nc - 1 - c, 0))
    blocked = jax.ShapeDtypeStruct((b, l // S5_BLOCK, width), BF16)
    state = [pltpu.VMEM((nb, S5_LANES), F32), pltpu.VMEM((nb, S5_LANES), F32), pltpu.VMEM((SUBLANES, S5_LANES), F32)]
    stage = pltpu.VMEM((S5_SLABS, nt, LANES), F32)
    f, r = sp
    wf = (perm, f["we"], f["t0"], f["wy"], f["are"], f["aim"])
    ut, ytf = pl.pallas_call(
        _s5_fwd_kernel,
        grid=(b, nc),
        in_specs=[tok_f] + [_resident(w) for w in wf],
        out_specs=[blk_f, blk_f],
        out_shape=[blocked, blocked],
        scratch_shapes=[stage] + state,
        compiler_params=_cparams(),
        name="s5fwd",
    )(u, *_operands(wf))
    wr = (perm_t, r["we"], r["t0"], r["wy"], r["are"], r["aim"])
    return pl.pallas_call(
        _s5_bwd_kernel,
        grid=(b, nc),
        in_specs=[blk_b, blk_b] + [_resident(w) for w in wr],
        out_specs=tok_b,
        out_shape=jax.ShapeDtypeStruct(u.shape, BF16),
        scratch_shapes=[stage, pltpu.VMEM((nb, width), BF16)] + state,
        compiler_params=_cparams(),
        name="s5bwd",
    )(ut, ytf, *_operands(wr))


def _ssd_chunk(xa, pcs_tl, pcs_hl2, dt_hl2, st_ref, d, reverse):
    q = xa.shape[0]
    hs = slice(d * M2_HEADS, (d + 1) * M2_HEADS)
    dt_hl = dt_hl2[hs, :]
    pcs_hl = pcs_hl2[hs, :]
    r = lax.broadcasted_iota(jnp.int32, (q, q), 0)
    c = lax.broadcasted_iota(jnp.int32, (q, q), 1)
    end = 0 if reverse else q - 1
    tot_hl = pcs_hl[:, end:end + 1]
    keep = (c >= r) if reverse else (c <= r)
    rowp_hl = pcs_hl - jnp.log2(dt_hl)
    wrow_hl = jnp.exp2(tot_hl - rowp_hl)
    dec_hl = jnp.exp2(tot_hl)

    lane = lax.broadcasted_iota(jnp.int32, (q, LANES), 1)
    first_head = lane < M2_HEADDIM
    first_head_row = lax.broadcasted_iota(jnp.int32, (1, LANES), 1) < M2_HEADDIM

    ys = []
    for g in range(M2_GROUPS):
        bm = xa[:, M2_INNER + g * M2_STATE:M2_INNER + (g + 1) * M2_STATE]
        cm = xa[:, M2_INNER + M2_GN + g * M2_STATE:M2_INNER + M2_GN + (g + 1) * M2_STATE]
        cb = lax.dot_general(cm, bm, (((1,), (1,)), ((), ())), preferred_element_type=F32)
        bt = bm.astype(F32).T
        st_g = st_ref[d, g]
        yoff = _dot(cm, st_g.astype(BF16))
        new_cols = []
        for jp in range(HEADS_PER_GROUP // 2):
            h0 = g * HEADS_PER_GROUP + 2 * jp
            pair = slice(h0 * M2_HEADDIM, (h0 + 2) * M2_HEADDIM)
            xp = xa[:, pair]
            xpf = xp.astype(F32)
            xbd = jnp.concatenate([jnp.where(first_head, xpf, 0.0), jnp.where(first_head, 0.0, xpf)],
                                  axis=0).astype(BF16)
            ms, ws, cols = [], [], []
            for h in (h0, h0 + 1):
                col = jnp.broadcast_to(pcs_tl[:, d * M2_HEADS + h:d * M2_HEADS + h + 1], (q, LANES))
                decay_dt = jnp.where(keep, jnp.exp2(jnp.tile(col, (1, q // LANES)) - rowp_hl[h:h + 1, :]), 0.0)
                ms.append((cb * decay_dt).astype(BF16))
                ws.append((bt * wrow_hl[h:h + 1, :]).astype(BF16))
                cols.append(col)
            m2 = jnp.concatenate(ms, axis=1)
            w2 = jnp.concatenate(ws, axis=1)
            yslab = yoff[:, 2 * jp * M2_HEADDIM:(2 * jp + 2) * M2_HEADDIM]
            ys.append(_dot(m2, xbd) + jnp.exp2(jnp.where(first_head, cols[0], cols[1])) * yslab)
            sslab = st_g[:, 2 * jp * M2_HEADDIM:(2 * jp + 2) * M2_HEADDIM]
            dec_row = jnp.where(first_head_row, dec_hl[h0:h0 + 1, :], dec_hl[h0 + 1:h0 + 2, :])
            new_cols.append(dec_row * sslab + _dot(w2, xbd))
        st_ref[d, g] = jnp.concatenate(new_cols, axis=1)
    return jnp.concatenate(ys, axis=1)


def _ssd_kernel(xf_ref, xb_ref, pf_ref, pb_ref, phf_ref, phb_ref, dthf_ref, dthb_ref, yf_ref, yb_ref, st_ref):
    @pl.when(pl.program_id(1) == 0)
    def _():
        st_ref[...] = jnp.zeros_like(st_ref)

    q = SSD_CHUNK
    n_sub = xf_ref.shape[1] // q
    for j in range(n_sub):
        rf = slice(j * q, (j + 1) * q)
        rb = slice((n_sub - 1 - j) * q, (n_sub - j) * q)
        yf_ref[0, rf, :] = _ssd_chunk(xf_ref[0, rf, :], pf_ref[0, rf, :], phf_ref[0, :, rf], dthf_ref[0, :, rf],
                                      st_ref, 0, False).astype(yf_ref.dtype)
        yb_ref[0, rb, :] = _ssd_chunk(xb_ref[0, rb, :], pb_ref[0, rb, :], phb_ref[0, :, rb], dthb_ref[0, :, rb],
                                      st_ref, 1, True).astype(yb_ref.dtype)


def _ssd(xact, pcs_tl, pcs_hl, dt_hl):
    b, l, _ = xact.shape
    q = SSD_CHUNK * SSD_STEP_CHUNKS
    nc = l // q
    fwd3 = lambda n: pl.BlockSpec((1, q, n), lambda i, c: (i, c, 0))
    bwd3 = lambda n: pl.BlockSpec((1, q, n), lambda i, c: (i, nc - 1 - c, 0))
    hl_f = pl.BlockSpec((1, N_DIR * M2_HEADS, q), lambda i, c: (i, 0, c))
    hl_b = pl.BlockSpec((1, N_DIR * M2_HEADS, q), lambda i, c: (i, 0, nc - 1 - c))
    return pl.pallas_call(
        _ssd_kernel,
        grid=(b, nc),
        in_specs=[fwd3(M2_CONV_DIM), bwd3(M2_CONV_DIM), fwd3(DT_PAD), bwd3(DT_PAD), hl_f, hl_b, hl_f, hl_b],
        out_specs=[fwd3(M2_INNER), bwd3(M2_INNER)],
        out_shape=[jax.ShapeDtypeStruct((b, l, M2_INNER), BF16)] * 2,
        scratch_shapes=[pltpu.VMEM((N_DIR, M2_GROUPS, M2_STATE, HEADS_PER_GROUP * M2_HEADDIM), F32)],
        compiler_params=_cparams(),
        name="ssd",
    )(xact, xact, pcs_tl, pcs_tl, pcs_hl, pcs_hl, dt_hl, dt_hl)


def _mix_kernel(x_ref, g_ref, u_ref, s5y_ref, xs_ref, mf_ref, mb_ref, z_ref,
                ds5_ref, wglu_ref, bglu_ref, ws5_ref, dm2_ref, nm2_ref, wm2_ref, wo_ref, o_ref):
    f = lambda ref: ref[...].astype(F32)
    y = ds5_ref[...] * f(u_ref) + f(s5y_ref)
    h = jax.nn.gelu(y)
    h = h * _sigmoid(_dot(h.astype(BF16), wglu_ref[...]) + bglu_ref[...])
    s5 = _dot(h.astype(BF16), ws5_ref[...])
    ym = dm2_ref[...] * f(xs_ref) + f(mf_ref) + f(mb_ref)
    gz = ym * _silu(f(z_ref))
    gw = M2_INNER // M2_GROUPS
    parts = []
    for i in range(M2_GROUPS):
        gs = gz[:, i * gw:(i + 1) * gw]
        parts.append(gs * lax.rsqrt(jnp.mean(gs * gs, axis=-1, keepdims=True) + EPS))
    gn = jnp.concatenate(parts, axis=1) * nm2_ref[...]
    m2 = _dot(gn.astype(BF16), wm2_ref[...])
    gates = f(g_ref)
    merged = gates[:, :D_MODEL] * s5 + gates[:, D_MODEL:] * m2
    o_ref[...] = x_ref[...] + _dot(merged.astype(BF16), wo_ref[...])


def _mix(x2d, gates, u, s5y, xact, mf, mb, z, mp):
    t = x2d.shape[0]
    tm = TM_MIX
    row = lambda n: pl.BlockSpec((tm, n), lambda i: (i, 0))
    weights = (mp["d_s5"], mp["w_glu"], mp["b_glu"], mp["w_s5_out"], mp["d_m2"], mp["m2_norm_w"],
               mp["w_m2_out"], mp["w_o"])
    return pl.pallas_call(
        _mix_kernel,
        grid=(t // tm,),
        in_specs=[row(D_MODEL), row(2 * D_MODEL), row(S5_WIDTH), row(S5_WIDTH),
                  row(M2_INNER), row(M2_INNER), row(M2_INNER), row(M2_INNER)]
        + [_resident(w) for w in weights],
        out_specs=row(D_MODEL),
        out_shape=jax.ShapeDtypeStruct((t, D_MODEL), F32),
        compiler_params=_cparams(),
        name="mix",
    )(x2d, gates, u, s5y, xact, mf, mb, z, *_operands(weights))


def _mlp_kernel(x_ref, n2_ref, wup_ref, wdn_ref, fn_ref, o_ref, *, final):
    x = x_ref[...]
    h = _rms(x, n2_ref[...]).astype(BF16)
    acc = x
    for c0 in range(0, D_FF, 2 * N_CHUNK_COLS):
        c1 = c0 + 2 * N_CHUNK_COLS
        hid = jnp.maximum(_dot(h, wup_ref[:, c0:c1]), 0.0)
        acc = acc + _dot((hid * hid).astype(BF16), wdn_ref[c0:c1, :])
    o_ref[...] = _rms(acc, fn_ref[...]) if final else acc


def _mlp(x2d, norm2_w, w_up, w_down, final_norm_w, final):
    t = x2d.shape[0]
    tm = TM_MLP
    row = pl.BlockSpec((tm, D_MODEL), lambda i: (i, 0))
    weights = (norm2_w, w_up, w_down, final_norm_w)
    return pl.pallas_call(
        functools.partial(_mlp_kernel, final=final),
        grid=(t // tm,),
        in_specs=[row] + [_resident(w) for w in weights],
        out_specs=row,
        out_shape=jax.ShapeDtypeStruct((t, D_MODEL), F32),
        compiler_params=_cparams(),
        name="mlp",
    )(x2d, *_operands(weights))


def _s5_perm():
    n = S5_SLAB_GROUPS * LANES
    i = jnp.arange(n)
    r_lo, gi, c = i // LANES, (i % LANES) // S5_GROUP, i % S5_GROUP
    dst = gi * LANES + r_lo * S5_GROUP + c
    return (dst[:, None] == jnp.arange(n)[None, :]).astype(BF16)


def _pair_rows_cols(m_re, m_im, rows_are_states):
    g, a, b = m_re.shape
    z = jnp.zeros_like(m_re[0::2])
    if rows_are_states:
        top = jnp.concatenate([m_re[0::2], z], axis=2), jnp.concatenate([z, m_re[1::2]], axis=2)
        bot = jnp.concatenate([m_im[0::2], z], axis=2), jnp.concatenate([z, m_im[1::2]], axis=2)
        return jnp.concatenate([top[0], top[1], bot[0], bot[1]], axis=1)
    left = jnp.concatenate([m_re[0::2], z, m_im[0::2], z], axis=2)
    right = jnp.concatenate([z, m_re[1::2], z, m_im[1::2]], axis=2)
    return jnp.concatenate([left, right], axis=1)


def _s5_params(lam_re, lam_im, log_dt, b_re, b_im, c_re, c_im):
    rb = S5_BLOCK
    delta = jnp.exp(log_dt)[..., None]
    mag = jnp.exp(lam_re * delta)
    a_re = mag * jnp.cos(lam_im * delta)
    a_im = mag * jnp.sin(lam_im * delta)
    inv = 1.0 / (lam_re * lam_re + lam_im * lam_im)
    q_re = ((a_re - 1.0) * lam_re + a_im * lam_im) * inv
    q_im = (a_im * lam_re - (a_re - 1.0) * lam_im) * inv
    bbar_re = q_re[..., None] * b_re - q_im[..., None] * b_im
    bbar_im = q_re[..., None] * b_im + q_im[..., None] * b_re
    kk = jnp.arange(rb + 1, dtype=F32).reshape(-1, 1, 1, 1)
    pk_mag = jnp.exp(kk * (lam_re * delta))
    pk_re = pk_mag * jnp.cos(kk * (lam_im * delta))
    pk_im = pk_mag * jnp.sin(kk * (lam_im * delta))
    r = jnp.arange(rb)
    out = []
    for d in range(N_DIR):
        fwd = d == 0
        pr, pi = pk_re[:, d], pk_im[:, d]
        w_re = pr[..., None] * bbar_re[d] - pi[..., None] * bbar_im[d]
        w_im = pr[..., None] * bbar_im[d] + pi[..., None] * bbar_re[d]
        k3 = kk[:rb, 0]
        ni_mag = jnp.exp(-k3 * (lam_re[d] * delta[d]))
        ni_re = ni_mag * jnp.cos(k3 * (lam_im[d] * delta[d]))
        ni_im = -ni_mag * jnp.sin(k3 * (lam_im[d] * delta[d]))
        src_re = ni_re[..., None] * bbar_re[d] - ni_im[..., None] * bbar_im[d]
        src_im = ni_re[..., None] * bbar_im[d] + ni_im[..., None] * bbar_re[d]
        dst_re = c_re[d][None] * pr[:rb, :, None, :] - c_im[d][None] * pi[:rb, :, None, :]
        dst_im = c_re[d][None] * pi[:rb, :, None, :] + c_im[d][None] * pr[:rb, :, None, :]
        if not fwd:
            src_re, src_im = w_re[:rb], w_im[:rb]
            dst_re = c_re[d][None] * ni_re[:, :, None, :] - c_im[d][None] * ni_im[:, :, None, :]
            dst_im = c_re[d][None] * ni_im[:, :, None, :] + c_im[d][None] * ni_re[:, :, None, :]
        n = rb * S5_GROUP
        src = jnp.transpose(jnp.concatenate([src_re, src_im], axis=2), (1, 0, 3, 2)).reshape(S5_GROUPS, n, -1)
        dst = jnp.transpose(jnp.concatenate([dst_re, -dst_im], axis=3), (1, 3, 0, 2)).reshape(S5_GROUPS, -1, n)
        t0 = jnp.matmul(src, dst, precision=lax.Precision.HIGHEST)
        causal = (r[None, :] >= r[:, None]) if fwd else (r[None, :] <= r[:, None])
        t0 = jnp.where(jnp.repeat(jnp.repeat(causal, S5_GROUP, axis=0), S5_GROUP, axis=1)[None], t0, 0.0)
        ke = (rb - 1 - r) if fwd else r
        we_re = jnp.transpose(w_re[ke], (1, 0, 3, 2)).reshape(S5_GROUPS, rb * S5_GROUP, S5_STATE)
        we_im = jnp.transpose(w_im[ke], (1, 0, 3, 2)).reshape(S5_GROUPS, rb * S5_GROUP, S5_STATE)
        ky = (r + 1) if fwd else (rb - r)
        g_re = c_re[d][None] * pr[ky][:, :, None, :] - c_im[d][None] * pi[ky][:, :, None, :]
        g_im = c_re[d][None] * pi[ky][:, :, None, :] + c_im[d][None] * pr[ky][:, :, None, :]
        wy_re = jnp.transpose(g_re, (1, 3, 0, 2)).reshape(S5_GROUPS, S5_STATE, rb * S5_GROUP)
        wy_im = jnp.transpose(-g_im, (1, 3, 0, 2)).reshape(S5_GROUPS, S5_STATE, rb * S5_GROUP)
        out.append({
            "we": _pair_rows_cols(we_re, we_im, False).astype(BF16),
            "t0": t0.astype(BF16),
            "wy": _pair_rows_cols(wy_re, wy_im, True).astype(BF16),
            "are": pr[rb].reshape(1, S5_LANES), "aim": pi[rb].reshape(1, S5_LANES),
        })
    return out


def _m2_params(dt_bias, a_log):
    pad = DT_PAD - N_DIR * M2_HEADS
    flat = lambda v: v.reshape(1, N_DIR * M2_HEADS)
    return {
        "dtb_tl": jnp.pad(flat(dt_bias), ((0, 0), (0, pad))),
        "alog_tl": jnp.pad(flat(a_log), ((0, 0), (0, pad))),
    }


def _pad_w_in(w_in):
    o_z = S5_WIDTH
    o_xbc = o_z + M2_INNER
    o_dt = o_xbc + M2_CONV_DIM
    o_g = o_dt + N_DIR * M2_HEADS
    w_in = w_in.astype(BF16)
    dt = jnp.pad(w_in[:, o_dt:o_g], ((0, 0), (0, DT_PAD - N_DIR * M2_HEADS)))
    return jnp.concatenate([w_in[:, :o_dt], dt, w_in[:, o_g:]], axis=1)


def _layer(x2d, b, l, lp, final_norm_w, final):
    u, z, xact, dt, gates, pcs = _inproj(x2d, l, lp["norm1_w"], lp["w_in"], lp["m2"]["dtb_tl"],
                                         lp["m2"]["alog_tl"], lp["conv_w"], lp["conv_b"])
    s5y = _s5_mixer_scan(u.reshape(b, l, S5_WIDTH), lp["s5"], lp["perm"], lp["perm_t"])
    xact = xact.reshape(b, l, M2_CONV_DIM)
    heads_on_sublanes = lambda v: jnp.swapaxes(v.reshape(b, l, DT_PAD)[:, :, :N_DIR * M2_HEADS], 1, 2)
    mf, mb = _ssd(xact, pcs.reshape(b, l, DT_PAD), heads_on_sublanes(pcs), heads_on_sublanes(dt))
    t = b * l
    x1 = _mix(x2d, gates, u, s5y.reshape(t, -1), xact.reshape(t, -1), mf.reshape(t, -1), mb.reshape(t, -1), z, lp)
    return _mlp(x1, lp["norm2_w"], lp["w_up"], lp["w_down"], final_norm_w, final)


def _trunk(x, layers, final_norm_w):
    b, l, _ = x.shape
    x2d = x.reshape(b * l, D_MODEL)
    for i, lp in enumerate(layers):
        x2d = _layer(x2d, b, l, lp, final_norm_w, i == len(layers) - 1)
    return x2d.reshape(b, l, D_MODEL)


def _prepare_layers(norm1_w, w_in, lam_re, lam_im, log_dt, b_re, b_im, c_re, c_im, d_s5, w_glu,
                    b_glu, w_s5_out, conv_w, conv_b, dt_bias, a_log, d_m2, m2_norm_w, w_m2_out,
                    w_o, norm2_w, w_up, w_down):
    depth = norm1_w.shape[0]
    row = lambda v: v.reshape(depth, 1, -1)
    stacked = {
        "norm1_w": row(norm1_w),
        "w_in": jax.vmap(_pad_w_in)(w_in),
        "s5": jax.vmap(_s5_params)(lam_re, lam_im, log_dt, b_re, b_im, c_re, c_im),
        "d_s5": row(d_s5),
        "w_glu": w_glu.astype(BF16),
        "b_glu": row(b_glu),
        "w_s5_out": w_s5_out.astype(BF16),
        "conv_w": conv_w,
        "conv_b": row(conv_b),
        "m2": jax.vmap(_m2_params)(dt_bias, a_log),
        "d_m2": row(jnp.repeat(d_m2, M2_HEADDIM, axis=1)),
        "m2_norm_w": row(m2_norm_w),
        "w_m2_out": w_m2_out.astype(BF16),
        "w_o": w_o.astype(BF16),
        "norm2_w": row(norm2_w),
        "w_up": w_up.astype(BF16),
        "w_down": w_down.astype(BF16),
    }
    perm = _s5_perm()
    layers = []
    for i in range(depth):
        lp = jax.tree.map(lambda v: _LayerWeight(v, i), stacked)
        lp["perm"], lp["perm_t"] = perm, perm.T
        layers.append(lp)
    return layers


def kernel(x_prompt, x_sample, norm1_w, w_in, lam_re, lam_im, log_dt, b_re, b_im, c_re, c_im, d_s5, w_glu, b_glu, w_s5_out, conv_w, conv_b, dt_bias, a_log, d_m2, m2_norm_w, w_m2_out, w_o, norm2_w, w_up, w_down, final_norm_w):
    layers = _prepare_layers(norm1_w, w_in, lam_re, lam_im, log_dt, b_re, b_im, c_re, c_im, d_s5,
                             w_glu, b_glu, w_s5_out, conv_w, conv_b, dt_bias, a_log, d_m2,
                             m2_norm_w, w_m2_out, w_o, norm2_w, w_up, w_down)
    fnw = final_norm_w.reshape(1, -1)
    return (_trunk(x_prompt, layers, fnw), _trunk(x_sample, layers, fnw))
```

```python
import functools
from typing import NamedTuple

import jax
import jax.numpy as jnp
from jax import lax
from jax.experimental import pallas as pl
from jax.experimental.pallas import tpu as pltpu

F32 = jnp.float32
BF16 = jnp.bfloat16

D_MODEL = 1024
DEPTH = 2
N_DIR = 2
EPS = 1e-6
S5_WIDTH = 768
S5_GROUP = 16
S5_GROUPS = 48
S5_STATE = 64
S5_LANES = S5_GROUPS * S5_STATE
M2_INNER = 1536
M2_HEADDIM = 64
M2_HEADS = 24
M2_GROUPS = 4
M2_STATE = 128
M2_CONV = 4
M2_GN = M2_GROUPS * M2_STATE
M2_CONV_DIM = M2_INNER + 2 * M2_GN
HEADS_PER_GROUP = M2_HEADS // M2_GROUPS
D_FF = 4096

LANES = 128
SUBLANES = 8
VMEM_LIMIT_BYTES = 56 * 1024 * 1024

LOG2E = 1.4426950408889634
GELU_C = 0.7978845608028654
DT_PAD = LANES
S5_SLAB_GROUPS = LANES // S5_GROUP
S5_SLABS = S5_WIDTH // LANES
S5_SLAB_STATES = S5_SLAB_GROUPS * S5_STATE

TM_INPROJ = 256
TM_MIX = 512
TM_MLP = 512
S5_BLOCK = 16
S5_ROWS = 128
SSD_CHUNK = 128
SSD_STEP_CHUNKS = 4
CONV_COLS = 256
CONV_SUB = 128
N_CHUNK_COLS = 512


def _cparams(flags=None):
    return pltpu.CompilerParams(dimension_semantics=None, vmem_limit_bytes=VMEM_LIMIT_BYTES, flags=flags)


class _LayerWeight(NamedTuple):
    stacked: jax.Array
    layer: int


def _resident(w):
    if isinstance(w, _LayerWeight):
        shape, layer = w.stacked.shape, w.layer
        return pl.BlockSpec((None,) + shape[1:], lambda *_: (layer,) + (0,) * (len(shape) - 1),
                            pipeline_mode=pl.Buffered(1))
    nd = w.ndim
    return pl.BlockSpec(w.shape, lambda *_: (0,) * nd, pipeline_mode=pl.Buffered(1))


def _operands(ws):
    return [w.stacked if isinstance(w, _LayerWeight) else w for w in ws]


def _silu(x):
    hx = 0.5 * x
    return hx * jnp.tanh(hx) + hx


def _sigmoid(x):
    return 0.5 * jnp.tanh(0.5 * x) + 0.5


def _gelu_tanh(x):
    hx = 0.5 * x
    inner = x * (GELU_C + (GELU_C * 0.044715) * (x * x))
    return hx * jnp.tanh(inner) + hx


def _softplus(x):
    return jnp.maximum(x, 0.0) + jnp.log1p(jnp.exp(-jnp.abs(x)))


def _rms(x, w):
    var = jnp.mean(x * x, axis=-1, keepdims=True)
    return x * lax.rsqrt(var + EPS) * w


def _dot(a, b):
    return jnp.dot(a, b, preferred_element_type=F32)


def _split3(x):
    hi = x.astype(BF16)
    r1 = x - hi.astype(F32)
    mid = r1.astype(BF16)
    lo = (r1 - mid.astype(F32)).astype(BF16)
    return hi, mid, lo


def _conv_silu_rows(xe, tm, first, last, w, b):
    before = jnp.where(first, 0.0, xe[tm:tm + SUBLANES])
    after = jnp.where(last, 0.0, xe[tm + SUBLANES:tm + 2 * SUBLANES])
    ext = jnp.concatenate([before, xe[:tm], after], axis=0)
    outs = []
    for r0 in range(0, tm, CONV_SUB):
        blk = ext[r0:r0 + CONV_SUB + 2 * SUBLANES]
        nr = blk.shape[0]
        mid = slice(SUBLANES, SUBLANES + CONV_SUB)
        acc = (w[0:1] * pltpu.roll(blk, 1, 0)[mid] + w[1:2] * blk[mid] + w[2:3] * pltpu.roll(blk, nr - 1, 0)[mid]
               + w[3:4] * pltpu.roll(blk, nr - 2, 0)[mid] + b)
        outs.append(_silu(acc))
    return jnp.concatenate(outs, axis=0)


def _inproj_kernel(x_ref, xp_ref, xn_ref, nw_ref, w_ref, dtb_ref, alog_ref, cw_ref, cb_ref,
                   u_ref, z_ref, xact_ref, dt_ref, g_ref, pcs_ref, *, tiles_per_seq):
    tm = x_ref.shape[0]
    i = pl.program_id(0)
    first = lax.rem(i, tiles_per_seq) == 0
    last = lax.rem(i + 1, tiles_per_seq) == 0
    he = _rms(jnp.concatenate([x_ref[...], xp_ref[...], xn_ref[...]], axis=0), nw_ref[...]).astype(BF16)
    h = he[:tm]
    ident = lambda v: v
    posts = (ident, ident, None, lambda v: _softplus(v + dtb_ref[...]), _sigmoid)
    off = 0
    for ref, post in zip((u_ref, z_ref, xact_ref, dt_ref, g_ref), posts):
        n = ref.shape[-1]
        step = CONV_COLS if post is None else N_CHUNK_COLS
        for c0 in range(0, n, step):
            c1 = min(c0 + step, n)
            wc = w_ref[:, off + c0:off + c1]
            if post is None:
                val = _conv_silu_rows(_dot(he, wc), tm, first, last, cw_ref[:, c0:c1], cb_ref[:, c0:c1])
            else:
                val = post(_dot(h, wc))
            ref[:, c0:c1] = val.astype(ref.dtype)
        off += n
    q = SSD_CHUNK
    r = lax.broadcasted_iota(jnp.int32, (q, q), 0)
    c = lax.broadcasted_iota(jnp.int32, (q, q), 1)
    low = jnp.where(c <= r, 1.0, 0.0).astype(BF16)
    upp = jnp.where(c >= r, 1.0, 0.0).astype(BF16)
    fwd_col = lax.broadcasted_iota(jnp.int32, (q, DT_PAD), 1) < M2_HEADS
    for r0 in range(0, dt_ref.shape[0], q):
        parts = _split3(dt_ref[r0:r0 + q, :] * (-jnp.exp(alog_ref[...])))
        cs_f = sum(_dot(low, p) for p in parts)
        cs_b = sum(_dot(upp, p) for p in parts)
        pcs_ref[r0:r0 + q, :] = jnp.where(fwd_col, cs_f, cs_b) * LOG2E


def _inproj(x2d, seq_len, norm_w, w_in_p, dtb_tl, alog_tl, conv_w, conv_b):
    t = x2d.shape[0]
    tm = TM_INPROJ
    assert tm % SSD_CHUNK == 0 and seq_len % tm == 0
    widths = (S5_WIDTH, M2_INNER, M2_CONV_DIM, DT_PAD, 2 * D_MODEL, DT_PAD)
    dtypes = (BF16, BF16, BF16, F32, BF16, F32)
    row = lambda n: pl.BlockSpec((tm, n), lambda i: (i, 0))
    n8 = tm // SUBLANES
    before = pl.BlockSpec((SUBLANES, D_MODEL), lambda i: (jnp.maximum(i * n8 - 1, 0), 0))
    after = pl.BlockSpec((SUBLANES, D_MODEL), lambda i: (jnp.minimum((i + 1) * n8, t // SUBLANES - 1), 0))
    consts = (norm_w, w_in_p, dtb_tl, alog_tl, conv_w, conv_b)
    return pl.pallas_call(
        functools.partial(_inproj_kernel, tiles_per_seq=seq_len // tm),
        grid=(t // tm,),
        in_specs=[row(D_MODEL), before, after] + [_resident(w) for w in consts],
        out_specs=[row(n) for n in widths],
        out_shape=[jax.ShapeDtypeStruct((t, n), dt) for n, dt in zip(widths, dtypes)],
        compiler_params=_cparams(),
        name="inproj",
    )(x2d, x2d, x2d, *_operands(consts))


def _s5_slab_halves():
    return [(k, h) for k in range(S5_SLABS) for h in range(2)]


S5_HALF = S5_BLOCK // 2
S5_GL = S5_BLOCK * S5_GROUP


def _s5_direction(ut_ref, we_ref, t0_ref, wy_ref, are_ref, aim_ref, ere_ref, eim_ref, car_ref, reverse, emit):
    nb = ut_ref.shape[1]
    ut = lambda lo, hi: ut_ref[0, :, lo:hi]

    @pl.when(pl.program_id(1) == 0)
    def _():
        car_ref[...] = jnp.zeros_like(car_ref)

    for gp in range(S5_GROUPS // 2):
        e = _dot(ut(gp * 2 * S5_GL, (gp + 1) * 2 * S5_GL), we_ref[gp])
        ere_ref[:, gp * LANES:(gp + 1) * LANES] = e[:, :LANES]
        eim_ref[:, gp * LANES:(gp + 1) * LANES] = e[:, LANES:]

    ar, ai = are_ref[...], aim_ref[...]

    def step(i, carry):
        sr, si = carry
        row = pl.ds(nb - 1 - i if reverse else i, 1)
        er, ei = ere_ref[row, :], eim_ref[row, :]
        ere_ref[row, :] = sr
        eim_ref[row, :] = si
        return ar * sr - ai * si + er, ar * si + ai * sr + ei

    sr, si = lax.fori_loop(0, nb, step, (car_ref[0:1, :], car_ref[1:2, :]), unroll=8)
    car_ref[0:1, :] = sr
    car_ref[1:2, :] = si

    for gp in range(S5_GROUPS // 2):
        sin = jnp.concatenate([ere_ref[:, gp * LANES:(gp + 1) * LANES], eim_ref[:, gp * LANES:(gp + 1) * LANES]],
                              axis=1).astype(BF16)
        yo = _dot(sin, wy_ref[gp])
        for gi in range(2):
            g = 2 * gp + gi
            emit(g, _dot(ut(g * S5_GL, (g + 1) * S5_GL), t0_ref[g]) + yo[:, gi * S5_GL:(gi + 1) * S5_GL])


def _s5_fwd_kernel(u_ref, p_ref, we_ref, t0_ref, wy_ref, are_ref, aim_ref, ut_ref, yt_ref,
                   stage_ref, ere_ref, eim_ref, car_ref):
    nb = u_ref.shape[1] // S5_BLOCK
    for k in range(S5_SLABS):
        stage_ref[k] = u_ref[0, :, k * LANES:(k + 1) * LANES].astype(F32)

    def offset_rows(k, r):
        return stage_ref[k, pl.ds(r, nb, stride=S5_BLOCK), :].astype(BF16)

    lhs = jnp.concatenate(
        [jnp.concatenate([offset_rows(k, S5_HALF * h + r) for r in range(S5_HALF)], axis=1)
         for k, h in _s5_slab_halves()], axis=0)
    perm = _dot(lhs, p_ref[...]).astype(BF16)
    for i, (k, h) in enumerate(_s5_slab_halves()):
        for gi in range(S5_SLAB_GROUPS):
            lo = (k * S5_SLAB_GROUPS + gi) * S5_GL + h * LANES
            ut_ref[0, :, lo:lo + LANES] = perm[i * nb:(i + 1) * nb, gi * LANES:(gi + 1) * LANES]

    def emit(g, yg):
        yt_ref[0, :, g * S5_GL:(g + 1) * S5_GL] = yg.astype(yt_ref.dtype)

    _s5_direction(ut_ref, we_ref, t0_ref, wy_ref, are_ref, aim_ref, ere_ref, eim_ref, car_ref, False, emit)


def _s5_bwd_kernel(ut_ref, ytf_ref, pt_ref, we_ref, t0_ref, wy_ref, are_ref, aim_ref, y_ref,
                   stage_ref, yt_ref, ere_ref, eim_ref, car_ref):
    nb = ut_ref.shape[1]

    def emit(g, yg):
        cols = slice(g * S5_GL, (g + 1) * S5_GL)
        yt_ref[:, cols] = (yg + ytf_ref[0, :, cols].astype(F32)).astype(yt_ref.dtype)

    _s5_direction(ut_ref, we_ref, t0_ref, wy_ref, are_ref, aim_ref, ere_ref, eim_ref, car_ref, True, emit)

    lhs = jnp.concatenate(
        [jnp.concatenate([yt_ref[:, (k * S5_SLAB_GROUPS + gi) * S5_GL + h * LANES:
                                 (k * S5_SLAB_GROUPS + gi) * S5_GL + (h + 1) * LANES]
                          for gi in range(S5_SLAB_GROUPS)], axis=1) for k, h in _s5_slab_halves()], axis=0)
    out = _dot(lhs, pt_ref[...])
    for i, (k, h) in enumerate(_s5_slab_halves()):
        for r in range(S5_HALF):
            stage_ref[k, pl.ds(S5_HALF * h + r, nb, stride=S5_BLOCK), :] = out[i * nb:(i + 1) * nb, r * LANES:(r + 1) * LANES]
    for k in range(S5_SLABS):
        y_ref[0, :, k * LANES:(k + 1) * LANES] = stage_ref[k].astype(y_ref.dtype)


def _s5_mixer_scan(u, sp, perm, perm_t):
    b, l, _ = u.shape
    nb = min(S5_ROWS, l // S5_BLOCK)
    nt = nb * S5_BLOCK
    nc = l // nt
    width = S5_BLOCK * S5_WIDTH
    tok_f = pl.BlockSpec((1, nt, S5_WIDTH), lambda i, c: (i, c, 0))
    blk_f = pl.BlockSpec((1, nb, width), lambda i, c: (i, c, 0))
    tok_b = pl.BlockSpec((1, nt, S5_WIDTH), lambda i, c: (i, nc - 1 - c, 0))
    blk_b = pl.BlockSpec((1, nb, width), lambda i, c: (i, nc - 1 - c, 0))
    blocked = jax.ShapeDtypeStruct((b, l // S5_BLOCK, width), BF16)
    state = [pltpu.VMEM((nb, S5_LANES), F32), pltpu.VMEM((nb, S5_LANES), F32), pltpu.VMEM((SUBLANES, S5_LANES), F32)]
    stage = pltpu.VMEM((S5_SLABS, nt, LANES), F32)
    f, r = sp
    wf = (perm, f["we"], f["t0"], f["wy"], f["are"], f["aim"])
    ut, ytf = pl.pallas_call(
        _s5_fwd_kernel,
        grid=(b, nc),
        in_specs=[tok_f] + [_resident(w) for w in wf],
        out_specs=[blk_f, blk_f],
        out_shape=[blocked, blocked],
        scratch_shapes=[stage] + state,
        compiler_params=_cparams(),
        name="s5fwd",
    )(u, *_operands(wf))
    wr = (perm_t, r["we"], r["t0"], r["wy"], r["are"], r["aim"])
    return pl.pallas_call(
        _s5_bwd_kernel,
        grid=(b, nc),
        in_specs=[blk_b, blk_b] + [_resident(w) for w in wr],
        out_specs=tok_b,
        out_shape=jax.ShapeDtypeStruct(u.shape, BF16),
        scratch_shapes=[stage, pltpu.VMEM((nb, width), BF16)] + state,
        compiler_params=_cparams(),
        name="s5bwd",
    )(ut, ytf, *_operands(wr))


def _ssd_chunk(xa, pcs_tl, pcs_hl2, dt_hl2, st_ref, d, reverse):
    q = xa.shape[0]
    hs = slice(d * M2_HEADS, (d + 1) * M2_HEADS)
    dt_hl = dt_hl2[hs, :]
    pcs_hl = pcs_hl2[hs, :]
    r = lax.broadcasted_iota(jnp.int32, (q, q), 0)
    c = lax.broadcasted_iota(jnp.int32, (q, q), 1)
    end = 0 if reverse else q - 1
    tot_hl = pcs_hl[:, end:end + 1]
    keep = (c >= r) if reverse else (c <= r)
    rowp_hl = pcs_hl - jnp.log2(dt_hl)
    wrow_hl = jnp.exp2(tot_hl - rowp_hl)
    dec_hl = jnp.exp2(tot_hl)

    lane = lax.broadcasted_iota(jnp.int32, (q, LANES), 1)
    first_head = lane < M2_HEADDIM
    first_head_row = lax.broadcasted_iota(jnp.int32, (1, LANES), 1) < M2_HEADDIM

    ys = []
    for g in range(M2_GROUPS):
        bm = xa[:, M2_INNER + g * M2_STATE:M2_INNER + (g + 1) * M2_STATE]
        cm = xa[:, M2_INNER + M2_GN + g * M2_STATE:M2_INNER + M2_GN + (g + 1) * M2_STATE]
        cb = lax.dot_general(cm, bm, (((1,), (1,)), ((), ())), preferred_element_type=F32)
        bt = bm.astype(F32).T
        st_g = st_ref[d, g]
        yoff = _dot(cm, st_g.astype(BF16))
        new_cols = []
        for jp in range(HEADS_PER_GROUP // 2):
            h0 = g * HEADS_PER_GROUP + 2 * jp
            pair = slice(h0 * M2_HEADDIM, (h0 + 2) * M2_HEADDIM)
            xp = xa[:, pair]
            xpf = xp.astype(F32)
            xbd = jnp.concatenate([jnp.where(first_head, xpf, 0.0), jnp.where(first_head, 0.0, xpf)],
                                  axis=0).astype(BF16)
            ms, ws, cols = [], [], []
            for h in (h0, h0 + 1):
                col = jnp.broadcast_to(pcs_tl[:, d * M2_HEADS + h:d * M2_HEADS + h + 1], (q, LANES))
                decay_dt = jnp.where(keep, jnp.exp2(jnp.tile(col, (1, q // LANES)) - rowp_hl[h:h + 1, :]), 0.0)
                ms.append((cb * decay_dt).astype(BF16))
                ws.append((bt * wrow_hl[h:h + 1, :]).astype(BF16))
                cols.append(col)
            m2 = jnp.concatenate(ms, axis=1)
            w2 = jnp.concatenate(ws, axis=1)
            yslab = yoff[:, 2 * jp * M2_HEADDIM:(2 * jp + 2) * M2_HEADDIM]
            ys.append(_dot(m2, xbd) + jnp.exp2(jnp.where(first_head, cols[0], cols[1])) * yslab)
            sslab = st_g[:, 2 * jp * M2_HEADDIM:(2 * jp + 2) * M2_HEADDIM]
            dec_row = jnp.where(first_head_row, dec_hl[h0:h0 + 1, :], dec_hl[h0 + 1:h0 + 2, :])
            new_cols.append(dec_row * sslab + _dot(w2, xbd))
        st_ref[d, g] = jnp.concatenate(new_cols, axis=1)
    return jnp.concatenate(ys, axis=1)


def _ssd_kernel(xf_ref, xb_ref, pf_ref, pb_ref, phf_ref, phb_ref, dthf_ref, dthb_ref, yf_ref, yb_ref, st_ref):
    @pl.when(pl.program_id(1) == 0)
    def _():
        st_ref[...] = jnp.zeros_like(st_ref)

    q = SSD_CHUNK
    n_sub = xf_ref.shape[1] // q
    for j in range(n_sub):
        rf = slice(j * q, (j + 1) * q)
        rb = slice((n_sub - 1 - j) * q, (n_sub - j) * q)
        yf_ref[0, rf, :] = _ssd_chunk(xf_ref[0, rf, :], pf_ref[0, rf, :], phf_ref[0, :, rf], dthf_ref[0, :, rf],
                                      st_ref, 0, False).astype(yf_ref.dtype)
        yb_ref[0, rb, :] = _ssd_chunk(xb_ref[0, rb, :], pb_ref[0, rb, :], phb_ref[0, :, rb], dthb_ref[0, :, rb],
                                      st_ref, 1, True).astype(yb_ref.dtype)


def _ssd(xact, pcs_tl, pcs_hl, dt_hl):
    b, l, _ = xact.shape
    q = SSD_CHUNK * SSD_STEP_CHUNKS
    nc = l // q
    fwd3 = lambda n: pl.BlockSpec((1, q, n), lambda i, c: (i, c, 0))
    bwd3 = lambda n: pl.BlockSpec((1, q, n), lambda i, c: (i, nc - 1 - c, 0))
    hl_f = pl.BlockSpec((1, N_DIR * M2_HEADS, q), lambda i, c: (i, 0, c))
    hl_b = pl.BlockSpec((1, N_DIR * M2_HEADS, q), lambda i, c: (i, 0, nc - 1 - c))
    return pl.pallas_call(
        _ssd_kernel,
        grid=(b, nc),
        in_specs=[fwd3(M2_CONV_DIM), bwd3(M2_CONV_DIM), fwd3(DT_PAD), bwd3(DT_PAD), hl_f, hl_b, hl_f, hl_b],
        out_specs=[fwd3(M2_INNER), bwd3(M2_INNER)],
        out_shape=[jax.ShapeDtypeStruct((b, l, M2_INNER), BF16)] * 2,
        scratch_shapes=[pltpu.VMEM((N_DIR, M2_GROUPS, M2_STATE, HEADS_PER_GROUP * M2_HEADDIM), F32)],
        compiler_params=_cparams(),
        name="ssd",
    )(xact, xact, pcs_tl, pcs_tl, pcs_hl, pcs_hl, dt_hl, dt_hl)


def _mix_kernel(x_ref, g_ref, u_ref, s5y_ref, xs_ref, mf_ref, mb_ref, z_ref,
                ds5_ref, wglu_ref, bglu_ref, ws5_ref, dm2_ref, nm2_ref, wm2_ref, wo_ref, o_ref):
    f = lambda ref: ref[...].astype(F32)
    y = ds5_ref[...] * f(u_ref) + f(s5y_ref)
    h = _gelu_tanh(y)
    h = h * _sigmoid(_dot(h.astype(BF16), wglu_ref[...]) + bglu_ref[...])
    s5 = _dot(h.astype(BF16), ws5_ref[...])
    ym = dm2_ref[...] * f(xs_ref) + f(mf_ref) + f(mb_ref)
    gz = ym * _silu(f(z_ref))
    gw = M2_INNER // M2_GROUPS
    parts = []
    for i in range(M2_GROUPS):
        gs = gz[:, i * gw:(i + 1) * gw]
        parts.append(gs * lax.rsqrt(jnp.mean(gs * gs, axis=-1, keepdims=True) + EPS))
    gn = jnp.concatenate(parts, axis=1) * nm2_ref[...]
    m2 = _dot(gn.astype(BF16), wm2_ref[...])
    gates = f(g_ref)
    merged = gates[:, :D_MODEL] * s5 + gates[:, D_MODEL:] * m2
    o_ref[...] = x_ref[...] + _dot(merged.astype(BF16), wo_ref[...])


def _mix(x2d, gates, u, s5y, xact, mf, mb, z, mp):
    t = x2d.shape[0]
    tm = TM_MIX
    row = lambda n: pl.BlockSpec((tm, n), lambda i: (i, 0))
    weights = (mp["d_s5"], mp["w_glu"], mp["b_glu"], mp["w_s5_out"], mp["d_m2"], mp["m2_norm_w"],
               mp["w_m2_out"], mp["w_o"])
    return pl.pallas_call(
        _mix_kernel,
        grid=(t // tm,),
        in_specs=[row(D_MODEL), row(2 * D_MODEL), row(S5_WIDTH), row(S5_WIDTH),
                  row(M2_INNER), row(M2_INNER), row(M2_INNER), row(M2_INNER)]
        + [_resident(w) for w in weights],
        out_specs=row(D_MODEL),
        out_shape=jax.ShapeDtypeStruct((t, D_MODEL), F32),
        compiler_params=_cparams(),
        name="mix",
    )(x2d, gates, u, s5y, xact, mf, mb, z, *_operands(weights))


def _mlp_kernel(x_ref, n2_ref, wup_ref, wdn_ref, fn_ref, o_ref, *, final):
    x = x_ref[...]
    h = _rms(x, n2_ref[...]).astype(BF16)
    acc = x
    for c0 in range(0, D_FF, 2 * N_CHUNK_COLS):
        c1 = c0 + 2 * N_CHUNK_COLS
        hid = jnp.maximum(_dot(h, wup_ref[:, c0:c1]), 0.0)
        acc = acc + _dot((hid * hid).astype(BF16), wdn_ref[c0:c1, :])
    o_ref[...] = _rms(acc, fn_ref[...]) if final else acc


def _mlp(x2d, norm2_w, w_up, w_down, final_norm_w, final):
    t = x2d.shape[0]
    tm = TM_MLP
    row = pl.BlockSpec((tm, D_MODEL), lambda i: (i, 0))
    weights = (norm2_w, w_up, w_down, final_norm_w)
    return pl.pallas_call(
        functools.partial(_mlp_kernel, final=final),
        grid=(t // tm,),
        in_specs=[row] + [_resident(w) for w in weights],
        out_specs=row,
        out_shape=jax.ShapeDtypeStruct((t, D_MODEL), F32),
        compiler_params=_cparams(),
        name="mlp",
    )(x2d, *_operands(weights))


def _s5_perm():
    n = S5_SLAB_GROUPS * LANES
    i = jnp.arange(n)
    r_lo, gi, c = i // LANES, (i % LANES) // S5_GROUP, i % S5_GROUP
    dst = gi * LANES + r_lo * S5_GROUP + c
    return (dst[:, None] == jnp.arange(n)[None, :]).astype(BF16)


def _pair_rows_cols(m_re, m_im, rows_are_states):
    g, a, b = m_re.shape
    z = jnp.zeros_like(m_re[0::2])
    if rows_are_states:
        top = jnp.concatenate([m_re[0::2], z], axis=2), jnp.concatenate([z, m_re[1::2]], axis=2)
        bot = jnp.concatenate([m_im[0::2], z], axis=2), jnp.concatenate([z, m_im[1::2]], axis=2)
        return jnp.concatenate([top[0], top[1], bot[0], bot[1]], axis=1)
    left = jnp.concatenate([m_re[0::2], z, m_im[0::2], z], axis=2)
    right = jnp.concatenate([z, m_re[1::2], z, m_im[1::2]], axis=2)
    return jnp.concatenate([left, right], axis=1)


def _s5_params(lam_re, lam_im, log_dt, b_re, b_im, c_re, c_im):
    rb = S5_BLOCK
    delta = jnp.exp(log_dt)[..., None]
    mag = jnp.exp(lam_re * delta)
    a_re = mag * jnp.cos(lam_im * delta)
    a_im = mag * jnp.sin(lam_im * delta)
    inv = 1.0 / (lam_re * lam_re + lam_im * lam_im)
    q_re = ((a_re - 1.0) * lam_re + a_im * lam_im) * inv
    q_im = (a_im * lam_re - (a_re - 1.0) * lam_im) * inv
    bbar_re = q_re[..., None] * b_re - q_im[..., None] * b_im
    bbar_im = q_re[..., None] * b_im + q_im[..., None] * b_re
    kk = jnp.arange(rb + 1, dtype=F32).reshape(-1, 1, 1, 1)
    pk_mag = jnp.exp(kk * (lam_re * delta))
    pk_re = pk_mag * jnp.cos(kk * (lam_im * delta))
    pk_im = pk_mag * jnp.sin(kk * (lam_im * delta))
    r = jnp.arange(rb)
    out = []
    for d in range(N_DIR):
        fwd = d == 0
        pr, pi = pk_re[:, d], pk_im[:, d]
        w_re = pr[..., None] * bbar_re[d] - pi[..., None] * bbar_im[d]
        w_im = pr[..., None] * bbar_im[d] + pi[..., None] * bbar_re[d]
        k3 = kk[:rb, 0]
        ni_mag = jnp.exp(-k3 * (lam_re[d] * delta[d]))
        ni_re = ni_mag * jnp.cos(k3 * (lam_im[d] * delta[d]))
        ni_im = -ni_mag * jnp.sin(k3 * (lam_im[d] * delta[d]))
        src_re = ni_re[..., None] * bbar_re[d] - ni_im[..., None] * bbar_im[d]
        src_im = ni_re[..., None] * bbar_im[d] + ni_im[..., None] * bbar_re[d]
        dst_re = c_re[d][None] * pr[:rb, :, None, :] - c_im[d][None] * pi[:rb, :, None, :]
        dst_im = c_re[d][None] * pi[:rb, :, None, :] + c_im[d][None] * pr[:rb, :, None, :]
        if not fwd:
            src_re, src_im = w_re[:rb], w_im[:rb]
            dst_re = c_re[d][None] * ni_re[:, :, None, :] - c_im[d][None] * ni_im[:, :, None, :]
            dst_im = c_re[d][None] * ni_im[:, :, None, :] + c_im[d][None] * ni_re[:, :, None, :]
        n = rb * S5_GROUP
        src = jnp.transpose(jnp.concatenate([src_re, src_im], axis=2), (1, 0, 3, 2)).reshape(S5_GROUPS, n, -1)
        dst = jnp.transpose(jnp.concatenate([dst_re, -dst_im], axis=3), (1, 3, 0, 2)).reshape(S5_GROUPS, -1, n)
        t0 = jnp.matmul(src, dst, precision=lax.Precision.HIGHEST)
        causal = (r[None, :] >= r[:, None]) if fwd else (r[None, :] <= r[:, None])
        t0 = jnp.where(jnp.repeat(jnp.repeat(causal, S5_GROUP, axis=0), S5_GROUP, axis=1)[None], t0, 0.0)
        ke = (rb - 1 - r) if fwd else r
        we_re = jnp.transpose(w_re[ke], (1, 0, 3, 2)).reshape(S5_GROUPS, rb * S5_GROUP, S5_STATE)
        we_im = jnp.transpose(w_im[ke], (1, 0, 3, 2)).reshape(S5_GROUPS, rb * S5_GROUP, S5_STATE)
        ky = (r + 1) if fwd else (rb - r)
        g_re = c_re[d][None] * pr[ky][:, :, None, :] - c_im[d][None] * pi[ky][:, :, None, :]
        g_im = c_re[d][None] * pi[ky][:, :, None, :] + c_im[d][None] * pr[ky][:, :, None, :]
        wy_re = jnp.transpose(g_re, (1, 3, 0, 2)).reshape(S5_GROUPS, S5_STATE, rb * S5_GROUP)
        wy_im = jnp.transpose(-g_im, (1, 3, 0, 2)).reshape(S5_GROUPS, S5_STATE, rb * S5_GROUP)
        out.append({
            "we": _pair_rows_cols(we_re, we_im, False).astype(BF16),
            "t0": t0.astype(BF16),
            "wy": _pair_rows_cols(wy_re, wy_im, True).astype(BF16),
            "are": pr[rb].reshape(1, S5_LANES), "aim": pi[rb].reshape(1, S5_LANES),
        })
    return out


def _m2_params(dt_bias, a_log):
    pad = DT_PAD - N_DIR * M2_HEADS
    flat = lambda v: v.reshape(1, N_DIR * M2_HEADS)
    return {
        "dtb_tl": jnp.pad(flat(dt_bias), ((0, 0), (0, pad))),
        "alog_tl": jnp.pad(flat(a_log), ((0, 0), (0, pad))),
    }


def _pad_w_in(w_in):
    o_z = S5_WIDTH
    o_xbc = o_z + M2_INNER
    o_dt = o_xbc + M2_CONV_DIM
    o_g = o_dt + N_DIR * M2_HEADS
    w_in = w_in.astype(BF16)
    dt = jnp.pad(w_in[:, o_dt:o_g], ((0, 0), (0, DT_PAD - N_DIR * M2_HEADS)))
    return jnp.concatenate([w_in[:, :o_dt], dt, w_in[:, o_g:]], axis=1)


def _layer(x2d, b, l, lp, final_norm_w, final):
    u, z, xact, dt, gates, pcs = _inproj(x2d, l, lp["norm1_w"], lp["w_in"], lp["m2"]["dtb_tl"],
                                         lp["m2"]["alog_tl"], lp["conv_w"], lp["conv_b"])
    s5y = _s5_mixer_scan(u.reshape(b, l, S5_WIDTH), lp["s5"], lp["perm"], lp["perm_t"])
    xact = xact.reshape(b, l, M2_CONV_DIM)
    heads_on_sublanes = lambda v: jnp.swapaxes(v.reshape(b, l, DT_PAD)[:, :, :N_DIR * M2_HEADS], 1, 2)
    mf, mb = _ssd(xact, pcs.reshape(b, l, DT_PAD), heads_on_sublanes(pcs), heads_on_sublanes(dt))
    t = b * l
    x1 = _mix(x2d, gates, u, s5y.reshape(t, -1), xact.reshape(t, -1), mf.reshape(t, -1), mb.reshape(t, -1), z, lp)
    return _mlp(x1, lp["norm2_w"], lp["w_up"], lp["w_down"], final_norm_w, final)


def _trunk(x, layers, final_norm_w):
    b, l, _ = x.shape
    x2d = x.reshape(b * l, D_MODEL)
    for i, lp in enumerate(layers):
        x2d = _layer(x2d, b, l, lp, final_norm_w, i == len(layers) - 1)
    return x2d.reshape(b, l, D_MODEL)


def _prepare_layers(norm1_w, w_in, lam_re, lam_im, log_dt, b_re, b_im, c_re, c_im, d_s5, w_glu,
                    b_glu, w_s5_out, conv_w, conv_b, dt_bias, a_log, d_m2, m2_norm_w, w_m2_out,
                    w_o, norm2_w, w_up, w_down):
    depth = norm1_w.shape[0]
    row = lambda v: v.reshape(depth, 1, -1)
    stacked = {
        "norm1_w": row(norm1_w),
        "w_in": jax.vmap(_pad_w_in)(w_in),
        "s5": jax.vmap(_s5_params)(lam_re, lam_im, log_dt, b_re, b_im, c_re, c_im),
        "d_s5": row(d_s5),
        "w_glu": w_glu.astype(BF16),
        "b_glu": row(b_glu),
        "w_s5_out": w_s5_out.astype(BF16),
        "conv_w": conv_w,
        "conv_b": row(conv_b),
        "m2": jax.vmap(_m2_params)(dt_bias, a_log),
        "d_m2": row(jnp.repeat(d_m2, M2_HEADDIM, axis=1)),
        "m2_norm_w": row(m2_norm_w),
        "w_m2_out": w_m2_out.astype(BF16),
        "w_o": w_o.astype(BF16),
        "norm2_w": row(norm2_w),
        "w_up": w_up.astype(BF16),
        "w_down": w_down.astype(BF16),
    }
    perm = _s5_perm()
    layers = []
    for i in range(depth):
        lp = jax.tree.map(lambda v: _LayerWeight(v, i), stacked)
        lp["perm"], lp["perm_t"] = perm, perm.T
        layers.append(lp)
    return layers


def kernel(x_prompt, x_sample, norm1_w, w_in, lam_re, lam_im, log_dt, b_re, b_im, c_re, c_im, d_s5, w_glu, b_glu, w_s5_out, conv_w, conv_b, dt_bias, a_log, d_m2, m2_norm_w, w_m2_out, w_o, norm2_w, w_up, w_down, final_norm_w):
    layers = _prepare_layers(norm1_w, w_in, lam_re, lam_im, log_dt, b_re, b_im, c_re, c_im, d_s5,
                             w_glu, b_glu, w_s5_out, conv_w, conv_b, dt_bias, a_log, d_m2,
                             m2_norm_w, w_m2_out, w_o, norm2_w, w_up, w_down)
    fnw = final_norm_w.reshape(1, -1)
    return (_trunk(x_prompt, layers, fnw), _trunk(x_sample, layers, fnw))
```

```python
import functools
from typing import NamedTuple

import jax
import jax.numpy as jnp
from jax import lax
from jax.experimental import pallas as pl
from jax.experimental.pallas import tpu as pltpu

F32 = jnp.float32
BF16 = jnp.bfloat16

D_MODEL = 1024
N_DIR = 2
EPS = 1e-6
S5_WIDTH = 768
S5_GROUP = 16
S5_GROUPS = 48
S5_STATE = 64
S5_LANES = S5_GROUPS * S5_STATE
M2_INNER = 1536
M2_HEADDIM = 64
M2_HEADS = 24
M2_GROUPS = 4
M2_STATE = 128
M2_GN = M2_GROUPS * M2_STATE
M2_CONV_DIM = M2_INNER + 2 * M2_GN
HEADS_PER_GROUP = M2_HEADS // M2_GROUPS
D_FF = 4096

LANES = 128
SUBLANES = 8
VMEM_LIMIT_BYTES = 56 * 1024 * 1024

LOG2E = 1.4426950408889634
GELU_C = 0.7978845608028654
DT_PAD = LANES
S5_SLAB_GROUPS = LANES // S5_GROUP
S5_SLABS = S5_WIDTH // LANES

TM_INPROJ = 256
TM_MIX = 512
TM_MLP = 512
S5_BLOCK = 16
S5_ROWS = 128
SSD_CHUNK = 128
SSD_STEP_CHUNKS = 4
CONV_COLS = 256
CONV_SUB = 128
N_CHUNK_COLS = 512


def _cparams(flags=None):
    return pltpu.CompilerParams(dimension_semantics=None, vmem_limit_bytes=VMEM_LIMIT_BYTES, flags=flags)


class _LayerWeight(NamedTuple):
    stacked: jax.Array
    layer: int


def _resident(w):
    if isinstance(w, _LayerWeight):
        shape, layer = w.stacked.shape, w.layer
        return pl.BlockSpec((None,) + shape[1:], lambda *_: (layer,) + (0,) * (len(shape) - 1),
                            pipeline_mode=pl.Buffered(1))
    nd = w.ndim
    return pl.BlockSpec(w.shape, lambda *_: (0,) * nd, pipeline_mode=pl.Buffered(1))


def _operands(ws):
    return [w.stacked if isinstance(w, _LayerWeight) else w for w in ws]


def _silu(x):
    hx = 0.5 * x
    return hx * jnp.tanh(hx) + hx


def _sigmoid(x):
    return 0.5 * jnp.tanh(0.5 * x) + 0.5


def _gelu_tanh(x):
    hx = 0.5 * x
    inner = x * (GELU_C + (GELU_C * 0.044715) * (x * x))
    return hx * jnp.tanh(inner) + hx


def _softplus(x):
    return jnp.maximum(x, 0.0) + jnp.log1p(jnp.exp(-jnp.abs(x)))


def _rms(x, w):
    var = jnp.mean(x * x, axis=-1, keepdims=True)
    return x * lax.rsqrt(var + EPS) * w


def _dot(a, b):
    return jnp.dot(a, b, preferred_element_type=F32)


def _split3(x):
    hi = x.astype(BF16)
    r1 = x - hi.astype(F32)
    mid = r1.astype(BF16)
    lo = (r1 - mid.astype(F32)).astype(BF16)
    return hi, mid, lo


def _conv_silu_rows(xe, tm, first, last, w, b):
    before = jnp.where(first, 0.0, xe[tm:tm + SUBLANES])
    after = jnp.where(last, 0.0, xe[tm + SUBLANES:tm + 2 * SUBLANES])
    ext = jnp.concatenate([before, xe[:tm], after], axis=0)
    outs = []
    for r0 in range(0, tm, CONV_SUB):
        blk = ext[r0:r0 + CONV_SUB + 2 * SUBLANES]
        nr = blk.shape[0]
        mid = slice(SUBLANES, SUBLANES + CONV_SUB)
        acc = (w[0:1] * pltpu.roll(blk, 1, 0)[mid] + w[1:2] * blk[mid] + w[2:3] * pltpu.roll(blk, nr - 1, 0)[mid]
               + w[3:4] * pltpu.roll(blk, nr - 2, 0)[mid] + b)
        outs.append(_silu(acc))
    return jnp.concatenate(outs, axis=0)


def _inproj_kernel(x_ref, xp_ref, xn_ref, nw_ref, w_ref, dtb_ref, alog_ref, cw_ref, cb_ref,
                   u_ref, z_ref, xact_ref, dt_ref, g_ref, pcs_ref, *, tiles_per_seq):
    tm = x_ref.shape[0]
    i = pl.program_id(0)
    first = lax.rem(i, tiles_per_seq) == 0
    last = lax.rem(i + 1, tiles_per_seq) == 0
    he = _rms(jnp.concatenate([x_ref[...], xp_ref[...], xn_ref[...]], axis=0), nw_ref[...]).astype(BF16)
    h = he[:tm]
    ident = lambda v: v
    posts = (ident, ident, None, lambda v: _softplus(v + dtb_ref[...]), _sigmoid)
    off = 0
    for ref, post in zip((u_ref, z_ref, xact_ref, dt_ref, g_ref), posts):
        n = ref.shape[-1]
        step = CONV_COLS if post is None else N_CHUNK_COLS
        for c0 in range(0, n, step):
            c1 = min(c0 + step, n)
            wc = w_ref[:, off + c0:off + c1]
            if post is None:
                val = _conv_silu_rows(_dot(he, wc), tm, first, last, cw_ref[:, c0:c1], cb_ref[:, c0:c1])
            else:
                val = post(_dot(h, wc))
            ref[:, c0:c1] = val.astype(ref.dtype)
        off += n
    q = SSD_CHUNK
    r = lax.broadcasted_iota(jnp.int32, (q, q), 0)
    c = lax.broadcasted_iota(jnp.int32, (q, q), 1)
    low = jnp.where(c <= r, 1.0, 0.0).astype(BF16)
    upp = jnp.where(c >= r, 1.0, 0.0).astype(BF16)
    fwd_col = lax.broadcasted_iota(jnp.int32, (q, DT_PAD), 1) < M2_HEADS
    for r0 in range(0, dt_ref.shape[0], q):
        parts = _split3(dt_ref[r0:r0 + q, :] * (-jnp.exp(alog_ref[...])))
        cs_f = sum(_dot(low, p) for p in parts)
        cs_b = sum(_dot(upp, p) for p in parts)
        pcs_ref[r0:r0 + q, :] = jnp.where(fwd_col, cs_f, cs_b) * LOG2E


def _inproj(x2d, seq_len, norm_w, w_in_p, dtb_tl, alog_tl, conv_w, conv_b):
    t = x2d.shape[0]
    tm = TM_INPROJ
    assert tm % SSD_CHUNK == 0 and seq_len % tm == 0
    widths = (S5_WIDTH, M2_INNER, M2_CONV_DIM, DT_PAD, 2 * D_MODEL, DT_PAD)
    dtypes = (BF16, BF16, BF16, F32, BF16, F32)
    row = lambda n: pl.BlockSpec((tm, n), lambda i: (i, 0))
    n8 = tm // SUBLANES
    before = pl.BlockSpec((SUBLANES, D_MODEL), lambda i: (jnp.maximum(i * n8 - 1, 0), 0))
    after = pl.BlockSpec((SUBLANES, D_MODEL), lambda i: (jnp.minimum((i + 1) * n8, t // SUBLANES - 1), 0))
    consts = (norm_w, w_in_p, dtb_tl, alog_tl, conv_w, conv_b)
    return pl.pallas_call(
        functools.partial(_inproj_kernel, tiles_per_seq=seq_len // tm),
        grid=(t // tm,),
        in_specs=[row(D_MODEL), before, after] + [_resident(w) for w in consts],
        out_specs=[row(n) for n in widths],
        out_shape=[jax.ShapeDtypeStruct((t, n), dt) for n, dt in zip(widths, dtypes)],
        compiler_params=_cparams(),
        name="inproj",
    )(x2d, x2d, x2d, *_operands(consts))


def _s5_slab_halves():
    return [(k, h) for k in range(S5_SLABS) for h in range(2)]


S5_HALF = S5_BLOCK // 2
S5_GL = S5_BLOCK * S5_GROUP


def _s5_direction(ut_ref, we_ref, t0_ref, wy_ref, are_ref, aim_ref, ere_ref, eim_ref, car_ref, reverse, emit):
    nb = ut_ref.shape[1]
    ut = lambda lo, hi: ut_ref[0, :, lo:hi]

    @pl.when(pl.program_id(1) == 0)
    def _():
        car_ref[...] = jnp.zeros_like(car_ref)

    for gp in range(S5_GROUPS // 2):
        e = _dot(ut(gp * 2 * S5_GL, (gp + 1) * 2 * S5_GL), we_ref[gp])
        ere_ref[:, gp * LANES:(gp + 1) * LANES] = e[:, :LANES]
        eim_ref[:, gp * LANES:(gp + 1) * LANES] = e[:, LANES:]

    ar, ai = are_ref[...], aim_ref[...]

    def step(i, carry):
        sr, si = carry
        row = pl.ds(nb - 1 - i if reverse else i, 1)
        er, ei = ere_ref[row, :], eim_ref[row, :]
        ere_ref[row, :] = sr
        eim_ref[row, :] = si
        return ar * sr - ai * si + er, ar * si + ai * sr + ei

    sr, si = lax.fori_loop(0, nb, step, (car_ref[0:1, :], car_ref[1:2, :]), unroll=8)
    car_ref[0:1, :] = sr
    car_ref[1:2, :] = si

    for gp in range(S5_GROUPS // 2):
        sin = jnp.concatenate([ere_ref[:, gp * LANES:(gp + 1) * LANES], eim_ref[:, gp * LANES:(gp + 1) * LANES]],
                              axis=1).astype(BF16)
        yo = _dot(sin, wy_ref[gp])
        for gi in range(2):
            g = 2 * gp + gi
            emit(g, _dot(ut(g * S5_GL, (g + 1) * S5_GL), t0_ref[g]) + yo[:, gi * S5_GL:(gi + 1) * S5_GL])


def _s5_fwd_kernel(u_ref, p_ref, we_ref, t0_ref, wy_ref, are_ref, aim_ref, ut_ref, yt_ref,
                   stage_ref, ere_ref, eim_ref, car_ref):
    nb = u_ref.shape[1] // S5_BLOCK
    for k in range(S5_SLABS):
        stage_ref[k] = u_ref[0, :, k * LANES:(k + 1) * LANES].astype(F32)

    def offset_rows(k, r):
        return stage_ref[k, pl.ds(r, nb, stride=S5_BLOCK), :].astype(BF16)

    lhs = jnp.concatenate(
        [jnp.concatenate([offset_rows(k, S5_HALF * h + r) for r in range(S5_HALF)], axis=1)
         for k, h in _s5_slab_halves()], axis=0)
    perm = _dot(lhs, p_ref[...]).astype(BF16)
    for i, (k, h) in enumerate(_s5_slab_halves()):
        for gi in range(S5_SLAB_GROUPS):
            lo = (k * S5_SLAB_GROUPS + gi) * S5_GL + h * LANES
            ut_ref[0, :, lo:lo + LANES] = perm[i * nb:(i + 1) * nb, gi * LANES:(gi + 1) * LANES]

    def emit(g, yg):
        yt_ref[0, :, g * S5_GL:(g + 1) * S5_GL] = yg.astype(yt_ref.dtype)

    _s5_direction(ut_ref, we_ref, t0_ref, wy_ref, are_ref, aim_ref, ere_ref, eim_ref, car_ref, False, emit)


def _s5_bwd_kernel(ut_ref, ytf_ref, pt_ref, we_ref, t0_ref, wy_ref, are_ref, aim_ref, y_ref,
                   stage_ref, yt_ref, ere_ref, eim_ref, car_ref):
    nb = ut_ref.shape[1]

    def emit(g, yg):
        cols = slice(g * S5_GL, (g + 1) * S5_GL)
        yt_ref[:, cols] = (yg + ytf_ref[0, :, cols].astype(F32)).astype(yt_ref.dtype)

    _s5_direction(ut_ref, we_ref, t0_ref, wy_ref, are_ref, aim_ref, ere_ref, eim_ref, car_ref, True, emit)

    lhs = jnp.concatenate(
        [jnp.concatenate([yt_ref[:, (k * S5_SLAB_GROUPS + gi) * S5_GL + h * LANES:
                                 (k * S5_SLAB_GROUPS + gi) * S5_GL + (h + 1) * LANES]
                          for gi in range(S5_SLAB_GROUPS)], axis=1) for k, h in _s5_slab_halves()], axis=0)
    out = _dot(lhs, pt_ref[...])
    for i, (k, h) in enumerate(_s5_slab_halves()):
        for r in range(S5_HALF):
            stage_ref[k, pl.ds(S5_HALF * h + r, nb, stride=S5_BLOCK), :] = out[i * nb:(i + 1) * nb, r * LANES:(r + 1) * LANES]
    for k in range(S5_SLABS):
        y_ref[0, :, k * LANES:(k + 1) * LANES] = stage_ref[k].astype(y_ref.dtype)


def _s5_mixer_scan(u, sp, perm, perm_t):
    b, l, _ = u.shape
    nb = min(S5_ROWS, l // S5_BLOCK)
    nt = nb * S5_BLOCK
    nc = l // nt
    width = S5_BLOCK * S5_WIDTH
    tok_f = pl.BlockSpec((1, nt, S5_WIDTH), lambda i, c: (i, c, 0))
    blk_f = pl.BlockSpec((1, nb, width), lambda i, c: (i, c, 0))
    tok_b = pl.BlockSpec((1, nt, S5_WIDTH), lambda i, c: (i, nc - 1 - c, 0))
    blk_b = pl.BlockSpec((1, nb, width), lambda i, c: (i, nc - 1 - c, 0))
    blocked = jax.ShapeDtypeStruct((b, l // S5_BLOCK, width), BF16)
    state = [pltpu.VMEM((nb, S5_LANES), F32), pltpu.VMEM((nb, S5_LANES), F32), pltpu.VMEM((SUBLANES, S5_LANES), F32)]
    stage = pltpu.VMEM((S5_SLABS, nt, LANES), F32)
    f, r = sp
    wf = (perm, f["we"], f["t0"], f["wy"], f["are"], f["aim"])
    ut, ytf = pl.pallas_call(
        _s5_fwd_kernel,
        grid=(b, nc),
        in_specs=[tok_f] + [_resident(w) for w in wf],
        out_specs=[blk_f, blk_f],
        out_shape=[blocked, blocked],
        scratch_shapes=[stage] + state,
        compiler_params=_cparams(),
        name="s5fwd",
    )(u, *_operands(wf))
    wr = (perm_t, r["we"], r["t0"], r["wy"], r["are"], r["aim"])
    return pl.pallas_call(
        _s5_bwd_kernel,
        grid=(b, nc),
        in_specs=[blk_b, blk_b] + [_resident(w) for w in wr],
        out_specs=tok_b,
        out_shape=jax.ShapeDtypeStruct(u.shape, BF16),
        scratch_shapes=[stage, pltpu.VMEM((nb, width), BF16)] + state,
        compiler_params=_cparams(),
        name="s5bwd",
    )(ut, ytf, *_operands(wr))


def _ssd_chunk(xa, pcs_tl, pcs_hl2, dt_hl2, st_ref, d, reverse):
    q = xa.shape[0]
    hs = slice(d * M2_HEADS, (d + 1) * M2_HEADS)
    dt_hl = dt_hl2[hs, :]
    pcs_hl = pcs_hl2[hs, :]
    r = lax.broadcasted_iota(jnp.int32, (q, q), 0)
    c = lax.broadcasted_iota(jnp.int32, (q, q), 1)
    end = 0 if reverse else q - 1
    tot_hl = pcs_hl[:, end:end + 1]
    keep = (c >= r) if reverse else (c <= r)
    rowp_hl = pcs_hl - jnp.log2(dt_hl)
    wrow_hl = jnp.exp2(tot_hl - rowp_hl)
    dec_hl = jnp.exp2(tot_hl)

    lane = lax.broadcasted_iota(jnp.int32, (q, LANES), 1)
    first_head = lane < M2_HEADDIM
    first_head_row = lax.broadcasted_iota(jnp.int32, (1, LANES), 1) < M2_HEADDIM

    ys = []
    for g in range(M2_GROUPS):
        bm = xa[:, M2_INNER + g * M2_STATE:M2_INNER + (g + 1) * M2_STATE]
        cm = xa[:, M2_INNER + M2_GN + g * M2_STATE:M2_INNER + M2_GN + (g + 1) * M2_STATE]
        cb = lax.dot_general(cm, bm, (((1,), (1,)), ((), ())), preferred_element_type=F32)
        bt = bm.astype(F32).T
        st_g = st_ref[d, g]
        yoff = _dot(cm, st_g.astype(BF16))
        new_cols = []
        for jp in range(HEADS_PER_GROUP // 2):
            h0 = g * HEADS_PER_GROUP + 2 * jp
            pair = slice(h0 * M2_HEADDIM, (h0 + 2) * M2_HEADDIM)
            xp = xa[:, pair]
            xpf = xp.astype(F32)
            xbd = jnp.concatenate([jnp.where(first_head, xpf, 0.0), jnp.where(first_head, 0.0, xpf)],
                                  axis=0).astype(BF16)
            ms, ws, cols = [], [], []
            for h in (h0, h0 + 1):
                col = jnp.broadcast_to(pcs_tl[:, d * M2_HEADS + h:d * M2_HEADS + h + 1], (q, LANES))
                decay_dt = jnp.where(keep, jnp.exp2(jnp.tile(col, (1, q // LANES)) - rowp_hl[h:h + 1, :]), 0.0)
                ms.append((cb * decay_dt).astype(BF16))
                ws.append((bt * wrow_hl[h:h + 1, :]).astype(BF16))
                cols.append(col)
            m2 = jnp.concatenate(ms, axis=1)
            w2 = jnp.concatenate(ws, axis=1)
            yslab = yoff[:, 2 * jp * M2_HEADDIM:(2 * jp + 2) * M2_HEADDIM]
            ys.append(_dot(m2, xbd) + jnp.exp2(jnp.where(first_head, cols[0], cols[1])) * yslab)
            sslab = st_g[:, 2 * jp * M2_HEADDIM:(2 * jp + 2) * M2_HEADDIM]
            dec_row = jnp.where(first_head_row, dec_hl[h0:h0 + 1, :], dec_hl[h0 + 1:h0 + 2, :])
            new_cols.append(dec_row * sslab + _dot(w2, xbd))
        st_ref[d, g] = jnp.concatenate(new_cols, axis=1)
    return jnp.concatenate(ys, axis=1)


def _ssd_kernel(xf_ref, xb_ref, pf_ref, pb_ref, phf_ref, phb_ref, dthf_ref, dthb_ref, yf_ref, yb_ref, st_ref):
    @pl.when(pl.program_id(1) == 0)
    def _():
        st_ref[...] = jnp.zeros_like(st_ref)

    q = SSD_CHUNK
    n_sub = xf_ref.shape[1] // q
    for j in range(n_sub):
        rf = slice(j * q, (j + 1) * q)
        rb = slice((n_sub - 1 - j) * q, (n_sub - j) * q)
        yf_ref[0, rf, :] = _ssd_chunk(xf_ref[0, rf, :], pf_ref[0, rf, :], phf_ref[0, :, rf], dthf_ref[0, :, rf],
                                      st_ref, 0, False).astype(yf_ref.dtype)
        yb_ref[0, rb, :] = _ssd_chunk(xb_ref[0, rb, :], pb_ref[0, rb, :], phb_ref[0, :, rb], dthb_ref[0, :, rb],
                                      st_ref, 1, True).astype(yb_ref.dtype)


def _ssd(xact, pcs_tl, pcs_hl, dt_hl):
    b, l, _ = xact.shape
    q = SSD_CHUNK * SSD_STEP_CHUNKS
    nc = l // q
    fwd3 = lambda n: pl.BlockSpec((1, q, n), lambda i, c: (i, c, 0))
    bwd3 = lambda n: pl.BlockSpec((1, q, n), lambda i, c: (i, nc - 1 - c, 0))
    hl_f = pl.BlockSpec((1, N_DIR * M2_HEADS, q), lambda i, c: (i, 0, c))
    hl_b = pl.BlockSpec((1, N_DIR * M2_HEADS, q), lambda i, c: (i, 0, nc - 1 - c))
    return pl.pallas_call(
        _ssd_kernel,
        grid=(b, nc),
        in_specs=[fwd3(M2_CONV_DIM), bwd3(M2_CONV_DIM), fwd3(DT_PAD), bwd3(DT_PAD), hl_f, hl_b, hl_f, hl_b],
        out_specs=[fwd3(M2_INNER), bwd3(M2_INNER)],
        out_shape=[jax.ShapeDtypeStruct((b, l, M2_INNER), BF16)] * 2,
        scratch_shapes=[pltpu.VMEM((N_DIR, M2_GROUPS, M2_STATE, HEADS_PER_GROUP * M2_HEADDIM), F32)],
        compiler_params=_cparams(),
        name="ssd",
    )(xact, xact, pcs_tl, pcs_tl, pcs_hl, pcs_hl, dt_hl, dt_hl)


def _mix_kernel(x_ref, g_ref, u_ref, s5y_ref, xs_ref, mf_ref, mb_ref, z_ref,
                ds5_ref, wglu_ref, bglu_ref, ws5_ref, dm2_ref, nm2_ref, wm2_ref, wo_ref, o_ref):
    f = lambda ref: ref[...].astype(F32)
    y = ds5_ref[...] * f(u_ref) + f(s5y_ref)
    h = _gelu_tanh(y)
    h = h * _sigmoid(_dot(h.astype(BF16), wglu_ref[...]) + bglu_ref[...])
    s5 = _dot(h.astype(BF16), ws5_ref[...])
    ym = dm2_ref[...] * f(xs_ref) + f(mf_ref) + f(mb_ref)
    gz = ym * _silu(f(z_ref))
    gw = M2_INNER // M2_GROUPS
    parts = []
    for i in range(M2_GROUPS):
        gs = gz[:, i * gw:(i + 1) * gw]
        parts.append(gs * lax.rsqrt(jnp.mean(gs * gs, axis=-1, keepdims=True) + EPS))
    gn = jnp.concatenate(parts, axis=1) * nm2_ref[...]
    m2 = _dot(gn.astype(BF16), wm2_ref[...])
    gates = f(g_ref)
    merged = gates[:, :D_MODEL] * s5 + gates[:, D_MODEL:] * m2
    o_ref[...] = x_ref[...] + _dot(merged.astype(BF16), wo_ref[...])


def _mix(x2d, gates, u, s5y, xact, mf, mb, z, mp):
    t = x2d.shape[0]
    tm = TM_MIX
    row = lambda n: pl.BlockSpec((tm, n), lambda i: (i, 0))
    weights = (mp["d_s5"], mp["w_glu"], mp["b_glu"], mp["w_s5_out"], mp["d_m2"], mp["m2_norm_w"],
               mp["w_m2_out"], mp["w_o"])
    return pl.pallas_call(
        _mix_kernel,
        grid=(t // tm,),
        in_specs=[row(D_MODEL), row(2 * D_MODEL), row(S5_WIDTH), row(S5_WIDTH),
                  row(M2_INNER), row(M2_INNER), row(M2_INNER), row(M2_INNER)]
        + [_resident(w) for w in weights],
        out_specs=row(D_MODEL),
        out_shape=jax.ShapeDtypeStruct((t, D_MODEL), F32),
        compiler_params=_cparams(),
        name="mix",
    )(x2d, gates, u, s5y, xact, mf, mb, z, *_operands(weights))


def _mlp_kernel(x_ref, n2_ref, wup_ref, wdn_ref, fn_ref, o_ref, *, final):
    x = x_ref[...]
    h = _rms(x, n2_ref[...]).astype(BF16)
    acc = x
    for c0 in range(0, D_FF, 2 * N_CHUNK_COLS):
        c1 = c0 + 2 * N_CHUNK_COLS
        hid = jnp.maximum(_dot(h, wup_ref[:, c0:c1]), 0.0)
        acc = acc + _dot((hid * hid).astype(BF16), wdn_ref[c0:c1, :])
    o_ref[...] = _rms(acc, fn_ref[...]) if final else acc


def _mlp(x2d, norm2_w, w_up, w_down, final_norm_w, final):
    t = x2d.shape[0]
    tm = TM_MLP
    row = pl.BlockSpec((tm, D_MODEL), lambda i: (i, 0))
    weights = (norm2_w, w_up, w_down, final_norm_w)
    return pl.pallas_call(
        functools.partial(_mlp_kernel, final=final),
        grid=(t // tm,),
        in_specs=[row] + [_resident(w) for w in weights],
        out_specs=row,
        out_shape=jax.ShapeDtypeStruct((t, D_MODEL), F32),
        compiler_params=_cparams(),
        name="mlp",
    )(x2d, *_operands(weights))


def _s5_perm():
    n = S5_SLAB_GROUPS * LANES
    i = jnp.arange(n)
    r_lo, gi, c = i // LANES, (i % LANES) // S5_GROUP, i % S5_GROUP
    dst = gi * LANES + r_lo * S5_GROUP + c
    return (dst[:, None] == jnp.arange(n)[None, :]).astype(BF16)


def _pair_rows_cols(m_re, m_im, rows_are_states):
    g, a, b = m_re.shape
    z = jnp.zeros_like(m_re[0::2])
    if rows_are_states:
        top = jnp.concatenate([m_re[0::2], z], axis=2), jnp.concatenate([z, m_re[1::2]], axis=2)
        bot = jnp.concatenate([m_im[0::2], z], axis=2), jnp.concatenate([z, m_im[1::2]], axis=2)
        return jnp.concatenate([top[0], top[1], bot[0], bot[1]], axis=1)
    left = jnp.concatenate([m_re[0::2], z, m_im[0::2], z], axis=2)
    right = jnp.concatenate([z, m_re[1::2], z, m_im[1::2]], axis=2)
    return jnp.concatenate([left, right], axis=1)


def _s5_params(lam_re, lam_im, log_dt, b_re, b_im, c_re, c_im):
    rb = S5_BLOCK
    delta = jnp.exp(log_dt)[..., None]
    mag = jnp.exp(lam_re * delta)
    a_re = mag * jnp.cos(lam_im * delta)
    a_im = mag * jnp.sin(lam_im * delta)
    inv = 1.0 / (lam_re * lam_re + lam_im * lam_im)
    q_re = ((a_re - 1.0) * lam_re + a_im * lam_im) * inv
    q_im = (a_im * lam_re - (a_re - 1.0) * lam_im) * inv
    bbar_re = q_re[..., None] * b_re - q_im[..., None] * b_im
    bbar_im = q_re[..., None] * b_im + q_im[..., None] * b_re
    kk = jnp.arange(rb + 1, dtype=F32).reshape(-1, 1, 1, 1)
    pk_mag = jnp.exp(kk * (lam_re * delta))
    pk_re = pk_mag * jnp.cos(kk * (lam_im * delta))
    pk_im = pk_mag * jnp.sin(kk * (lam_im * delta))
    r = jnp.arange(rb)
    out = []
    for d in range(N_DIR):
        fwd = d == 0
        pr, pi = pk_re[:, d], pk_im[:, d]
        w_re = pr[..., None] * bbar_re[d] - pi[..., None] * bbar_im[d]
        w_im = pr[..., None] * bbar_im[d] + pi[..., None] * bbar_re[d]
        k3 = kk[:rb, 0]
        ni_mag = jnp.exp(-k3 * (lam_re[d] * delta[d]))
        ni_re = ni_mag * jnp.cos(k3 * (lam_im[d] * delta[d]))
        ni_im = -ni_mag * jnp.sin(k3 * (lam_im[d] * delta[d]))
        src_re = ni_re[..., None] * bbar_re[d] - ni_im[..., None] * bbar_im[d]
        src_im = ni_re[..., None] * bbar_im[d] + ni_im[..., None] * bbar_re[d]
        dst_re = c_re[d][None] * pr[:rb, :, None, :] - c_im[d][None] * pi[:rb, :, None, :]
        dst_im = c_re[d][None] * pi[:rb, :, None, :] + c_im[d][None] * pr[:rb, :, None, :]
        if not fwd:
            src_re, src_im = w_re[:rb], w_im[:rb]
            dst_re = c_re[d][None] * ni_re[:, :, None, :] - c_im[d][None] * ni_im[:, :, None, :]
            dst_im = c_re[d][None] * ni_im[:, :, None, :] + c_im[d][None] * ni_re[:, :, None, :]
        n = rb * S5_GROUP
        src = jnp.transpose(jnp.concatenate([src_re, src_im], axis=2), (1, 0, 3, 2)).reshape(S5_GROUPS, n, -1)
        dst = jnp.transpose(jnp.concatenate([dst_re, -dst_im], axis=3), (1, 3, 0, 2)).reshape(S5_GROUPS, -1, n)
        t0 = jnp.matmul(src, dst, precision=lax.Precision.HIGHEST)
        causal = (r[None, :] >= r[:, None]) if fwd else (r[None, :] <= r[:, None])
        t0 = jnp.where(jnp.repeat(jnp.repeat(causal, S5_GROUP, axis=0), S5_GROUP, axis=1)[None], t0, 0.0)
        ke = (rb - 1 - r) if fwd else r
        we_re = jnp.transpose(w_re[ke], (1, 0, 3, 2)).reshape(S5_GROUPS, rb * S5_GROUP, S5_STATE)
        we_im = jnp.transpose(w_im[ke], (1, 0, 3, 2)).reshape(S5_GROUPS, rb * S5_GROUP, S5_STATE)
        ky = (r + 1) if fwd else (rb - r)
        g_re = c_re[d][None] * pr[ky][:, :, None, :] - c_im[d][None] * pi[ky][:, :, None, :]
        g_im = c_re[d][None] * pi[ky][:, :, None, :] + c_im[d][None] * pr[ky][:, :, None, :]
        wy_re = jnp.transpose(g_re, (1, 3, 0, 2)).reshape(S5_GROUPS, S5_STATE, rb * S5_GROUP)
        wy_im = jnp.transpose(-g_im, (1, 3, 0, 2)).reshape(S5_GROUPS, S5_STATE, rb * S5_GROUP)
        out.append({
            "we": _pair_rows_cols(we_re, we_im, False).astype(BF16),
            "t0": t0.astype(BF16),
            "wy": _pair_rows_cols(wy_re, wy_im, True).astype(BF16),
            "are": pr[rb].reshape(1, S5_LANES), "aim": pi[rb].reshape(1, S5_LANES),
        })
    return out


def _m2_params(dt_bias, a_log):
    pad = DT_PAD - N_DIR * M2_HEADS
    flat = lambda v: v.reshape(1, N_DIR * M2_HEADS)
    return {
        "dtb_tl": jnp.pad(flat(dt_bias), ((0, 0), (0, pad))),
        "alog_tl": jnp.pad(flat(a_log), ((0, 0), (0, pad))),
    }


def _pad_w_in(w_in):
    o_z = S5_WIDTH
    o_xbc = o_z + M2_INNER
    o_dt = o_xbc + M2_CONV_DIM
    o_g = o_dt + N_DIR * M2_HEADS
    w_in = w_in.astype(BF16)
    dt = jnp.pad(w_in[:, o_dt:o_g], ((0, 0), (0, DT_PAD - N_DIR * M2_HEADS)))
    return jnp.concatenate([w_in[:, :o_dt], dt, w_in[:, o_g:]], axis=1)


def _layer(x2d, b, l, lp, final_norm_w, final):
    u, z, xact, dt, gates, pcs = _inproj(x2d, l, lp["norm1_w"], lp["w_in"], lp["m2"]["dtb_tl"],
                                         lp["m2"]["alog_tl"], lp["conv_w"], lp["conv_b"])
    s5y = _s5_mixer_scan(u.reshape(b, l, S5_WIDTH), lp["s5"], lp["perm"], lp["perm_t"])
    xact = xact.reshape(b, l, M2_CONV_DIM)
    heads_on_sublanes = lambda v: jnp.swapaxes(v.reshape(b, l, DT_PAD)[:, :, :N_DIR * M2_HEADS], 1, 2)
    mf, mb = _ssd(xact, pcs.reshape(b, l, DT_PAD), heads_on_sublanes(pcs), heads_on_sublanes(dt))
    t = b * l
    x1 = _mix(x2d, gates, u, s5y.reshape(t, -1), xact.reshape(t, -1), mf.reshape(t, -1), mb.reshape(t, -1), z, lp)
    return _mlp(x1, lp["norm2_w"], lp["w_up"], lp["w_down"], final_norm_w, final)


def _trunk(x, layers, final_norm_w):
    b, l, _ = x.shape
    x2d = x.reshape(b * l, D_MODEL)
    for i, lp in enumerate(layers):
        x2d = _layer(x2d, b, l, lp, final_norm_w, i == len(layers) - 1)
    return x2d.reshape(b, l, D_MODEL)


def _prepare_layers(norm1_w, w_in, lam_re, lam_im, log_dt, b_re, b_im, c_re, c_im, d_s5, w_glu,
                    b_glu, w_s5_out, conv_w, conv_b, dt_bias, a_log, d_m2, m2_norm_w, w_m2_out,
                    w_o, norm2_w, w_up, w_down):
    depth = norm1_w.shape[0]
    row = lambda v: v.reshape(depth, 1, -1)
    stacked = {
        "norm1_w": row(norm1_w),
        "w_in": jax.vmap(_pad_w_in)(w_in),
        "s5": jax.vmap(_s5_params)(lam_re, lam_im, log_dt, b_re, b_im, c_re, c_im),
        "d_s5": row(d_s5),
        "w_glu": w_glu.astype(BF16),
        "b_glu": row(b_glu),
        "w_s5_out": w_s5_out.astype(BF16),
        "conv_w": conv_w,
        "conv_b": row(conv_b),
        "m2": jax.vmap(_m2_params)(dt_bias, a_log),
        "d_m2": row(jnp.repeat(d_m2, M2_HEADDIM, axis=1)),
        "m2_norm_w": row(m2_norm_w),
        "w_m2_out": w_m2_out.astype(BF16),
        "w_o": w_o.astype(BF16),
        "norm2_w": row(norm2_w),
        "w_up": w_up.astype(BF16),
        "w_down": w_down.astype(BF16),
    }
    perm = _s5_perm()
    layers = []
    for i in range(depth):
        lp = jax.tree.map(lambda v: _LayerWeight(v, i), stacked)
        lp["perm"], lp["perm_t"] = perm, perm.T
        layers.append(lp)
    return layers


def kernel(x_prompt, x_sample, norm1_w, w_in, lam_re, lam_im, log_dt, b_re, b_im, c_re, c_im, d_s5, w_glu, b_glu, w_s5_out, conv_w, conv_b, dt_bias, a_log, d_m2, m2_norm_w, w_m2_out, w_o, norm2_w, w_up, w_down, final_norm_w):
    layers = _prepare_layers(norm1_w, w_in, lam_re, lam_im, log_dt, b_re, b_im, c_re, c_im, d_s5,
                             w_glu, b_glu, w_s5_out, conv_w, conv_b, dt_bias, a_log, d_m2,
                             m2_norm_w, w_m2_out, w_o, norm2_w, w_up, w_down)
    fnw = final_norm_w.reshape(1, -1)
    return (_trunk(x_prompt, layers, fnw), _trunk(x_sample, layers, fnw))
```
